```python
import jax, jax.numpy as jnp
from jax import lax
import numpy as np


D_MODEL = 4096
BATCH = 1
SEQ = 16384
DEPTH = 2

MLA_HEADS = 16
MLA_Q_RANK = 1024
MLA_KV_RANK = 512
MLA_NOPE_DIM = 128
MLA_ROPE_DIM = 64
MLA_V_DIM = 128
MLA_QK_DIM = MLA_NOPE_DIM + MLA_ROPE_DIM
MLA_WIDTH = MLA_HEADS * MLA_V_DIM
ROPE_THETA = 10000.0
FOX_HEADS = 16
FOX_HEAD_DIM = 128
FOX_WIDTH = FOX_HEADS * FOX_HEAD_DIM
FOX_FORGET_BIAS_INIT = 3.0
QUERY_BLOCK = 128
PEER_HEADS = 8
PEER_N_KEYS = 128
PEER_N_EXPERTS = PEER_N_KEYS * PEER_N_KEYS
PEER_KEY_DIM = 256
PEER_HALF = PEER_KEY_DIM // 2
PEER_TOPK = 16
PEER_TOKEN_BLOCK = 64
PLE_DIM = 256
DEEPNORM_ALPHA = (2 * DEPTH) ** 0.25
DEEPNORM_BETA = (8 * DEPTH) ** -0.25
LN_EPS = 1e-5
RMS_EPS = 1e-6
IN_SIZES = (MLA_Q_RANK, MLA_KV_RANK, MLA_ROPE_DIM, FOX_WIDTH, FOX_WIDTH, FOX_WIDTH, FOX_HEADS, D_MODEL, D_MODEL)
IN_COLS = sum(IN_SIZES)

kernel_name = 'hybrid_mla_fox_peer_deepnorm'


def layer_norm(x, g, b):
    xf = x.astype(jnp.float32)
    mu = jnp.mean(xf, axis=-1, keepdims=True)
    xc = xf - mu
    var = jnp.mean(xc * xc, axis=-1, keepdims=True)
    y = xc * lax.rsqrt(var + LN_EPS) * g.astype(jnp.float32) + b.astype(jnp.float32)
    return y.astype(x.dtype)


def rms_norm(x, g):
    xf = x.astype(jnp.float32)
    y = xf * lax.rsqrt(jnp.mean(xf * xf, axis=-1, keepdims=True) + RMS_EPS) * g.astype(jnp.float32)
    return y.astype(x.dtype)


def rope(x, positions):
    half = x.shape[-1] // 2
    inv_freq = ROPE_THETA ** (-jnp.arange(half, dtype=jnp.float32) / half)
    ang = positions.astype(jnp.float32)[:, :, None, None] * inv_freq
    cos, sin = jnp.cos(ang), jnp.sin(ang)
    x1 = x[..., :half].astype(jnp.float32)
    x2 = x[..., half:].astype(jnp.float32)
    return jnp.concatenate([x1 * cos - x2 * sin, x2 * cos + x1 * sin], axis=-1).astype(x.dtype)


def causal_block_attention(q, k, v, scale, cum=None):
    B, S, H, Dk = q.shape
    nb = S // QUERY_BLOCK
    q_blocks = q.reshape(B, nb, QUERY_BLOCK, H, Dk).transpose(1, 0, 2, 3, 4)
    c_blocks = None if cum is None else cum.reshape(B, H, nb, QUERY_BLOCK).transpose(2, 0, 1, 3)
    starts = jnp.arange(nb, dtype=jnp.int32) * QUERY_BLOCK
    k_pos = jnp.arange(S, dtype=jnp.int32)

    def one_block(args):
        q_blk, c_blk, start = args
        s = jnp.einsum('bqhd,bkhd->bhqk', q_blk, k, preferred_element_type=jnp.float32) * scale
        if c_blk is not None:
            s = s + (c_blk[:, :, :, None] - cum[:, :, None, :])
        q_pos = start + jnp.arange(QUERY_BLOCK, dtype=jnp.int32)
        s = jnp.where(k_pos[None, :] <= q_pos[:, None], s, -jnp.inf)
        w = jax.nn.softmax(s, axis=-1).astype(v.dtype)
        return jnp.einsum('bhqk,bkhd->bqhd', w, v)

    out = lax.map(one_block, (q_blocks, c_blocks, starts))
    return out.transpose(1, 0, 2, 3, 4).reshape(B, S, H, v.shape[-1])


def token_mixer(h, positions, w_in, g_cq, g_ckv, w_uq, w_ukv, b_f, w_ya, w_yb, w_o):
    B, S, _ = h.shape
    z = h @ w_in
    split_at = np.cumsum(IN_SIZES)[:-1].tolist()
    c_q, c_kv, k_r, f_q, f_k, f_v, f_logit, gate_a, gate_b = jnp.split(z, split_at, axis=-1)

    q = (rms_norm(c_q, g_cq) @ w_uq).reshape(B, S, MLA_HEADS, MLA_QK_DIM)
    q_mla = jnp.concatenate([q[..., :MLA_NOPE_DIM], rope(q[..., MLA_NOPE_DIM:], positions)], axis=-1)
    kv = (rms_norm(c_kv, g_ckv) @ w_ukv).reshape(B, S, MLA_HEADS, MLA_NOPE_DIM + MLA_V_DIM)
    k_nope, v_mla = kv[..., :MLA_NOPE_DIM], kv[..., MLA_NOPE_DIM:]
    k_rope = rope(k_r[:, :, None, :], positions)
    k_mla = jnp.concatenate([k_nope, jnp.broadcast_to(k_rope, (B, S, MLA_HEADS, MLA_ROPE_DIM))], axis=-1)
    y_a = causal_block_attention(q_mla, k_mla, v_mla, MLA_QK_DIM ** -0.5).reshape(B, S, MLA_WIDTH)

    log_f = jax.nn.log_sigmoid((f_logit + b_f).astype(jnp.float32))
    cum = jnp.cumsum(log_f, axis=1).transpose(0, 2, 1)
    y_b = causal_block_attention(f_q.reshape(B, S, FOX_HEADS, FOX_HEAD_DIM),
                                 f_k.reshape(B, S, FOX_HEADS, FOX_HEAD_DIM),
                                 f_v.reshape(B, S, FOX_HEADS, FOX_HEAD_DIM),
                                 FOX_HEAD_DIM ** -0.5, cum).reshape(B, S, FOX_WIDTH)

    merged = jax.nn.sigmoid(gate_a) * (y_a @ w_ya) + jax.nn.sigmoid(gate_b) * (y_b @ w_yb)
    return merged @ w_o


def peer_ffn(h, w_pq, sub_k1, sub_k2, u_tab, v_tab):
    B, S, D = h.shape
    n = B * S
    hf = h.reshape(n, D)
    q = (hf @ w_pq).reshape(n, PEER_HEADS, 2, PEER_HALF)
    s1 = jnp.einsum('nhc,hjc->nhj', q[:, :, 0], sub_k1, preferred_element_type=jnp.float32)
    s2 = jnp.einsum('nhc,hjc->nhj', q[:, :, 1], sub_k2, preferred_element_type=jnp.float32)
    t1, i1 = lax.top_k(s1, PEER_TOPK)
    t2, i2 = lax.top_k(s2, PEER_TOPK)
    cand = (t1[..., :, None] + t2[..., None, :]).reshape(n, PEER_HEADS, PEER_TOPK * PEER_TOPK)
    cand_idx = (i1[..., :, None] * PEER_N_KEYS + i2[..., None, :]).reshape(n, PEER_HEADS, PEER_TOPK * PEER_TOPK)
    top, pos = lax.top_k(cand, PEER_TOPK)
    idx = jnp.take_along_axis(cand_idx, pos, axis=-1)
    g = jax.nn.softmax(top, axis=-1).astype(h.dtype)

    nb = n // PEER_TOKEN_BLOCK

    def one_block(args):
        hb, ib, gb = args
        act = jnp.einsum('thkd,td->thk', u_tab[ib], hb)
        wgt = gb * jax.nn.gelu(act)
        return jnp.einsum('thk,thkd->td', wgt, v_tab[ib])

    out = lax.map(one_block, (hf.reshape(nb, PEER_TOKEN_BLOCK, D),
                              idx.reshape(nb, PEER_TOKEN_BLOCK, PEER_HEADS, PEER_TOPK),
                              g.reshape(nb, PEER_TOKEN_BLOCK, PEER_HEADS, PEER_TOPK)))
    return out.reshape(B, S, D)


def setup_inputs(seed: int = 0) -> dict:
    key = jax.random.key(seed)
    ks = jax.random.split(key, 24)

    def nrm(k, shape, scale):
        return jax.random.normal(k, shape, dtype=jnp.float32) * scale

    L = DEPTH
    x = nrm(ks[0], (BATCH, SEQ, D_MODEL), 1.0)
    p = nrm(ks[1], (DEPTH, BATCH, SEQ, PLE_DIM), 1.0)
    positions = jnp.broadcast_to(jnp.arange(SEQ, dtype=jnp.int32), (BATCH, SEQ))
    w_in = nrm(ks[2], (L, D_MODEL, IN_COLS), D_MODEL ** -0.5)
    g_cq = 1.0 + nrm(ks[3], (L, MLA_Q_RANK), 0.02)
    g_ckv = 1.0 + nrm(ks[4], (L, MLA_KV_RANK), 0.02)
    w_uq = nrm(ks[5], (L, MLA_Q_RANK, MLA_HEADS * MLA_QK_DIM), MLA_Q_RANK ** -0.5)
    w_ukv = nrm(ks[6], (L, MLA_KV_RANK, MLA_HEADS * (MLA_NOPE_DIM + MLA_V_DIM)), MLA_KV_RANK ** -0.5)
    b_f = FOX_FORGET_BIAS_INIT + nrm(ks[7], (L, FOX_HEADS), 0.1)
    w_ya = nrm(ks[8], (L, MLA_WIDTH, D_MODEL), DEEPNORM_BETA * MLA_WIDTH ** -0.5)
    w_yb = nrm(ks[9], (L, FOX_WIDTH, D_MODEL), DEEPNORM_BETA * FOX_WIDTH ** -0.5)
    w_o = nrm(ks[10], (L, D_MODEL, D_MODEL), DEEPNORM_BETA * D_MODEL ** -0.5)
    ln1_g = 1.0 + nrm(ks[11], (L, D_MODEL), 0.02)
    ln1_b = nrm(ks[12], (L, D_MODEL), 0.02)
    w_pq = nrm(ks[13], (L, D_MODEL, PEER_HEADS * PEER_KEY_DIM), D_MODEL ** -0.5)
    sub_k1 = nrm(ks[14], (L, PEER_HEADS, PEER_N_KEYS, PEER_HALF), PEER_HALF ** -0.5)
    sub_k2 = nrm(ks[15], (L, PEER_HEADS, PEER_N_KEYS, PEER_HALF), PEER_HALF ** -0.5)
    u_tab = nrm(ks[16], (L, PEER_N_EXPERTS, D_MODEL), D_MODEL ** -0.5)
    v_tab = nrm(ks[17], (L, PEER_N_EXPERTS, D_MODEL), DEEPNORM_BETA)
    w_pg = nrm(ks[18], (L, D_MODEL, D_MODEL), D_MODEL ** -0.5)
    w_pe = nrm(ks[19], (L, PLE_DIM, D_MODEL), DEEPNORM_BETA * PLE_DIM ** -0.5)
    ln2_g = 1.0 + nrm(ks[20], (L, D_MODEL), 0.02)
    ln2_b = nrm(ks[21], (L, D_MODEL), 0.02)
    return {'x': x, 'p': p, 'positions': positions, 'w_in': w_in, 'g_cq': g_cq, 'g_ckv': g_ckv,
            'w_uq': w_uq, 'w_ukv': w_ukv, 'b_f': b_f, 'w_ya': w_ya, 'w_yb': w_yb, 'w_o': w_o,
            'ln1_g': ln1_g, 'ln1_b': ln1_b, 'w_pq': w_pq, 'sub_k1': sub_k1, 'sub_k2': sub_k2,
            'u_tab': u_tab, 'v_tab': v_tab, 'w_pg': w_pg, 'w_pe': w_pe, 'ln2_g': ln2_g, 'ln2_b': ln2_b}


def reference(x, p, positions, w_in, g_cq, g_ckv, w_uq, w_ukv, b_f, w_ya, w_yb, w_o,
              ln1_g, ln1_b, w_pq, sub_k1, sub_k2, u_tab, v_tab, w_pg, w_pe, ln2_g, ln2_b):
    h = x
    for i in range(DEPTH):
        mix = token_mixer(h, positions, w_in[i], g_cq[i], g_ckv[i], w_uq[i], w_ukv[i], b_f[i],
                          w_ya[i], w_yb[i], w_o[i])
        h = layer_norm(DEEPNORM_ALPHA * h + mix, ln1_g[i], ln1_b[i])
        ffn = peer_ffn(h, w_pq[i], sub_k1[i], sub_k2[i], u_tab[i], v_tab[i])
        ple = jax.nn.sigmoid(h @ w_pg[i]) * (p[i] @ w_pe[i])
        h = layer_norm(DEEPNORM_ALPHA * h + ffn + ple, ln2_g[i], ln2_b[i])
    return h
```

```python
import functools
import math

import numpy as np
import jax
import jax.numpy as jnp
from jax import lax
from jax.experimental import pallas as pl
from jax.experimental.pallas import tpu as pltpu

F32 = jnp.float32
BF16 = jnp.bfloat16

MLA_HEADS = 16
MLA_Q_RANK = 1024
MLA_KV_RANK = 512
MLA_NOPE = 128
MLA_ROPE = 64
MLA_V = 128
ROPE_THETA = 10000.0
FOX_HEADS = 16
FOX_DIM = 128
PEER_HEADS = 8
PEER_KEYS = 128
PEER_HALF = 128
PEER_TOPK = 16
LN_EPS = 1e-5
RMS_EPS = 1e-6
LANES = 128
NEG = -1e30

VMEM_LIMIT = 56 * 1024 * 1024


def _cparams(n_axes):
    return pltpu.CompilerParams(dimension_semantics=("arbitrary",) * n_axes,
                                vmem_limit_bytes=VMEM_LIMIT)


def _mm_kernel(x_ref, w_ref, o_ref):
    o_ref[...] = jnp.dot(x_ref[...], w_ref[...], preferred_element_type=F32).astype(o_ref.dtype)


def _mm_scale_kernel(x_ref, w_ref, s_ref, o_ref):
    acc = jnp.dot(x_ref[...], w_ref[...], preferred_element_type=F32)
    o_ref[...] = (acc * s_ref[...]).astype(o_ref.dtype)


def _mm(x, w, out_dtype, tm, tn, colscale=None, name="mm"):
    m, k = x.shape
    _, n = w.shape
    tm, tn = min(tm, m), min(tn, n)
    assert m % tm == 0 and n % tn == 0
    in_specs = [pl.BlockSpec((tm, k), lambda i, j: (i, 0)),
                pl.BlockSpec((k, tn), lambda i, j: (0, j))]
    args = [x, w]
    body = _mm_kernel
    if colscale is not None:
        in_specs.append(pl.BlockSpec((1, tn), lambda i, j: (0, j)))
        args.append(colscale)
        body = _mm_scale_kernel
    return pl.pallas_call(
        body,
        grid=(m // tm, n // tn),
        in_specs=in_specs,
        out_specs=pl.BlockSpec((tm, tn), lambda i, j: (i, j)),
        out_shape=jax.ShapeDtypeStruct((m, n), out_dtype),
        compiler_params=_cparams(2),
        name=name,
    )(*args)


def _rope_table_kernel(pos_ref, c_ref, cos_ref, sina_ref, sinb_ref):
    ang = pos_ref[...].astype(F32) * c_ref[0:1, :]
    cos = jnp.cos(ang)
    sin = jnp.sin(ang)
    cos_ref[...] = cos * c_ref[1:2, :]
    sina_ref[...] = sin * c_ref[2:3, :]
    sinb_ref[...] = sin * c_ref[3:4, :]


def _rope_tables(positions_col):
    n = positions_col.shape[0]
    tm = min(1024, n)
    half = MLA_ROPE // 2
    inv_freq = (ROPE_THETA ** (-np.arange(half, dtype=np.float32) / half)).astype(np.float32)
    consts = np.zeros((8, LANES), np.float32)
    consts[0, :half] = inv_freq
    consts[0, half:2 * half] = inv_freq
    consts[1, :2 * half] = 1.0
    consts[2, :half] = -1.0
    consts[3, half:2 * half] = 1.0
    shp = jax.ShapeDtypeStruct((n, LANES), F32)
    return pl.pallas_call(
        _rope_table_kernel,
        grid=(n // tm,),
        in_specs=[pl.BlockSpec((tm, 1), lambda i: (i, 0)),
                  pl.BlockSpec((8, LANES), lambda i: (0, 0))],
        out_specs=[pl.BlockSpec((tm, LANES), lambda i: (i, 0))] * 3,
        out_shape=[shp, shp, shp],
        compiler_params=_cparams(1),
        name="rope_tables",
    )(positions_col, jnp.asarray(consts))


def _rope128(x, cos, sina, sinb):
    return (x * cos + pltpu.roll(x, 3 * (MLA_ROPE // 2), 1) * sina
            + pltpu.roll(x, MLA_ROPE // 2, 1) * sinb)


def _rms(x, g):
    return x * lax.rsqrt(jnp.mean(x * x, axis=-1, keepdims=True) + RMS_EPS) * g


def _mla_q_kernel(cq_ref, g_ref, wn_ref, wr_ref, cos_ref, sina_ref, sinb_ref, q_ref, *, scale):
    xn = _rms(cq_ref[...], g_ref[...]).astype(BF16)
    qn = jnp.dot(xn, wn_ref[...], preferred_element_type=F32) * scale
    qr = jnp.dot(xn, wr_ref[...], preferred_element_type=F32) * scale
    cos, sina, sinb = cos_ref[...], sina_ref[...], sinb_ref[...]
    for hh in range(2):
        lo = hh * LANES
        q_ref[:, 2 * lo:2 * lo + LANES] = qn[:, lo:lo + LANES].astype(q_ref.dtype)
        q_ref[:, 2 * lo + LANES:2 * lo + 2 * LANES] = _rope128(
            qr[:, lo:lo + LANES], cos, sina, sinb).astype(q_ref.dtype)


def _mla_q(zs, g_cq, wqn, wqr, tabs):
    n = zs.shape[0]
    tm = min(512, n)
    pairs = MLA_HEADS // 2
    scale = (MLA_NOPE + MLA_ROPE) ** -0.5
    tab_spec = pl.BlockSpec((tm, LANES), lambda i, p: (i, 0))
    return pl.pallas_call(
        functools.partial(_mla_q_kernel, scale=scale),
        grid=(n // tm, pairs),
        in_specs=[pl.BlockSpec((tm, MLA_Q_RANK), lambda i, p: (i, 0)),
                  pl.BlockSpec((1, MLA_Q_RANK), lambda i, p: (0, 0)),
                  pl.BlockSpec((MLA_Q_RANK, 2 * LANES), lambda i, p: (0, p)),
                  pl.BlockSpec((MLA_Q_RANK, 2 * LANES), lambda i, p: (0, p)),
                  tab_spec, tab_spec, tab_spec],
        out_specs=pl.BlockSpec((tm, 4 * LANES), lambda i, p: (i, p)),
        out_shape=jax.ShapeDtypeStruct((n, MLA_HEADS * 2 * LANES), BF16),
        compiler_params=_cparams(2),
        name="mla_q",
    )(zs, g_cq, wqn, wqr, *tabs)


def _mla_kv_kernel(ckv_ref, g_ref, kr_ref, wk_ref, wv_ref, cos_ref, sina_ref, sinb_ref,
                   k_ref, v_ref):
    xn = _rms(ckv_ref[...], g_ref[...]).astype(BF16)
    kn = jnp.dot(xn, wk_ref[...], preferred_element_type=F32)
    v_ref[...] = jnp.dot(xn, wv_ref[...], preferred_element_type=F32).astype(v_ref.dtype)
    kr = _rope128(kr_ref[...], cos_ref[...], sina_ref[...], sinb_ref[...]).astype(k_ref.dtype)
    for hh in range(2):
        lo = hh * LANES
        k_ref[:, 2 * lo:2 * lo + LANES] = kn[:, lo:lo + LANES].astype(k_ref.dtype)
        k_ref[:, 2 * lo + LANES:2 * lo + 2 * LANES] = kr


def _mla_kv(zs, g_ckv, wuk, wuv, tabs):
    n = zs.shape[0]
    tm = min(512, n)
    pairs = MLA_HEADS // 2
    ckv_blk = MLA_Q_RANK // MLA_KV_RANK
    kr_blk = (MLA_Q_RANK + MLA_KV_RANK) // LANES
    tab_spec = pl.BlockSpec((tm, LANES), lambda i, p: (i, 0))
    return pl.pallas_call(
        _mla_kv_kernel,
        grid=(n // tm, pairs),
        in_specs=[pl.BlockSpec((tm, MLA_KV_RANK), lambda i, p: (i, ckv_blk)),
                  pl.BlockSpec((1, MLA_KV_RANK), lambda i, p: (0, 0)),
                  pl.BlockSpec((tm, LANES), lambda i, p: (i, kr_blk)),
                  pl.BlockSpec((MLA_KV_RANK, 2 * LANES), lambda i, p: (0, p)),
                  pl.BlockSpec((MLA_KV_RANK, 2 * LANES), lambda i, p: (0, p)),
                  tab_spec, tab_spec, tab_spec],
        out_specs=[pl.BlockSpec((tm, 4 * LANES), lambda i, p: (i, p)),
                   pl.BlockSpec((tm, 2 * LANES), lambda i, p: (i, p))],
        out_shape=[jax.ShapeDtypeStruct((n, MLA_HEADS * 2 * LANES), BF16),
                   jax.ShapeDtypeStruct((n, MLA_HEADS * MLA_V), BF16)],
        compiler_params=_cparams(2),
        name="mla_kv",
    )(zs, g_ckv, zs, wuk, wuv, *tabs)


def _split3(x):
    hi = x.astype(BF16)
    r = x - hi.astype(F32)
    mid = r.astype(BF16)
    lo = (r - mid.astype(F32)).astype(BF16)
    return hi, mid, lo


def _cum_kernel(fl_ref, bf_ref, tri_ref, o_ref, carry_ref):
    @pl.when(pl.program_id(0) == 0)
    def _():
        carry_ref[...] = jnp.zeros_like(carry_ref)

    x = fl_ref[...] + bf_ref[...]
    lf = jnp.minimum(x, 0.0) - jnp.log1p(jnp.exp(-jnp.abs(x)))
    tri = tri_ref[...]
    c = carry_ref[...]
    for part in _split3(lf):
        c = c + jnp.dot(tri, part, preferred_element_type=F32)
    carry_ref[...] = c[c.shape[0] - 1:, :]
    o_ref[...] = c.T[:o_ref.shape[0], :]


def _fox_cum(zs, bf_row):
    n = zs.shape[0]
    tm = 256
    fl_blk = (MLA_Q_RANK + MLA_KV_RANK) // LANES + 1
    tri = jnp.asarray(np.tril(np.ones((tm, tm), np.float32)), dtype=BF16)
    return pl.pallas_call(
        _cum_kernel,
        grid=(n // tm,),
        in_specs=[pl.BlockSpec((tm, LANES), lambda i: (i, fl_blk)),
                  pl.BlockSpec((1, LANES), lambda i: (0, 0)),
                  pl.BlockSpec((tm, tm), lambda i: (0, 0))],
        out_specs=pl.BlockSpec((FOX_HEADS, tm), lambda i: (0, i)),
        out_shape=jax.ShapeDtypeStruct((FOX_HEADS, n), F32),
        scratch_shapes=[pltpu.VMEM((1, LANES), F32)],
        compiler_params=_cparams(1),
        name="fox_cum",
    )(zs, bf_row, tri)


def _attn_kernel(*refs, tq, fox):
    if fox:
        q_ref, k_ref, v_ref, ck_ref, o_ref, m_sc, l_sc, acc_sc = refs
    else:
        q_ref, k_ref, v_ref, o_ref, m_sc, l_sc, acc_sc = refs
    i = pl.program_id(1)
    m_sc[...] = jnp.full_like(m_sc, NEG)
    l_sc[...] = jnp.zeros_like(l_sc)
    acc_sc[...] = jnp.zeros_like(acc_sc)
    q = q_ref[...]

    def chunk(j, masked):
        off = pl.multiple_of(j * tq, tq)
        k = k_ref[pl.ds(off, tq), :]
        s = lax.dot_general(q, k, (((1,), (1,)), ((), ())), preferred_element_type=F32)
        if fox:
            s = s - ck_ref[0, :, pl.ds(off, tq)]
        if masked:
            row = lax.broadcasted_iota(jnp.int32, (tq, tq), 0)
            col = lax.broadcasted_iota(jnp.int32, (tq, tq), 1)
            s = jnp.where(col <= row, s, NEG)
        m_prev = m_sc[...]
        m_new = jnp.maximum(m_prev, jnp.max(s, axis=1, keepdims=True))
        alpha = jnp.exp(m_prev - m_new)
        p = jnp.exp(s - m_new)
        l_sc[...] = alpha * l_sc[...] + jnp.sum(p, axis=1, keepdims=True)
        acc_sc[...] = alpha * acc_sc[...] + jnp.dot(
            p.astype(BF16), v_ref[pl.ds(off, tq), :], preferred_element_type=F32)
        m_sc[...] = m_new

    def body(j, carry):
        chunk(j, False)
        return carry

    lax.fori_loop(0, i, body, 0)
    chunk(i, True)
    o_ref[...] = (acc_sc[...] / l_sc[...]).astype(o_ref.dtype)


def _attention(q_arr, k_arr, v_arr, heads, dk, dv, q_blk0, k_blk0, v_blk0, ck=None):
    n = q_arr.shape[0]
    tq = min(512, n)
    in_specs = [pl.BlockSpec((tq, dk), lambda h, i: (i, q_blk0 + h)),
                pl.BlockSpec((n, dk), lambda h, i: (0, k_blk0 + h)),
                pl.BlockSpec((n, dv), lambda h, i: (0, v_blk0 + h))]
    args = [q_arr, k_arr, v_arr]
    if ck is not None:
        in_specs.append(pl.BlockSpec((1, 1, n), lambda h, i: (h, 0, 0)))
        args.append(ck)
    return pl.pallas_call(
        functools.partial(_attn_kernel, tq=tq, fox=ck is not None),
        grid=(heads, n // tq),
        in_specs=in_specs,
        out_specs=pl.BlockSpec((tq, dv), lambda h, i: (i, h)),
        out_shape=jax.ShapeDtypeStruct((n, heads * dv), BF16),
        scratch_shapes=[pltpu.VMEM((tq, 1), F32), pltpu.VMEM((tq, 1), F32),
                        pltpu.VMEM((tq, dv), F32)],
        compiler_params=_cparams(2),
        name="attn_fox" if ck is not None else "attn_mla",
    )(*args)


def _merge_kernel(ya_ref, yb_ref, wa_ref, wb_ref, ga_ref, gb_ref, o_ref):
    a = jnp.dot(ya_ref[...], wa_ref[...], preferred_element_type=F32)
    b = jnp.dot(yb_ref[...], wb_ref[...], preferred_element_type=F32)
    ga = jax.nn.sigmoid(ga_ref[...].astype(F32))
    gb = jax.nn.sigmoid(gb_ref[...].astype(F32))
    o_ref[...] = (ga * a + gb * b).astype(o_ref.dtype)


def _merge(ya, yb, w_ya, w_yb, zb, ga_col0, gb_col0):
    n, ka = ya.shape
    kb = yb.shape[1]
    d = w_ya.shape[1]
    tm, tn = min(1024, n), 512
    ga0, gb0 = ga_col0 // tn, gb_col0 // tn
    return pl.pallas_call(
        _merge_kernel,
        grid=(n // tm, d // tn),
        in_specs=[pl.BlockSpec((tm, ka), lambda i, j: (i, 0)),
                  pl.BlockSpec((tm, kb), lambda i, j: (i, 0)),
                  pl.BlockSpec((ka, tn), lambda i, j: (0, j)),
                  pl.BlockSpec((kb, tn), lambda i, j: (0, j)),
                  pl.BlockSpec((tm, tn), lambda i, j: (i, ga0 + j)),
                  pl.BlockSpec((tm, tn), lambda i, j: (i, gb0 + j))],
        out_specs=pl.BlockSpec((tm, tn), lambda i, j: (i, j)),
        out_shape=jax.ShapeDtypeStruct((n, d), BF16),
        compiler_params=_cparams(2),
        name="merge",
    )(ya, yb, w_ya, w_yb, zb, zb)


def _ln_kernel(*refs, alpha, n_add):
    h_ref = refs[0]
    add_refs = refs[1:1 + n_add]
    g_ref, b_ref, o_ref, ob_ref = refs[1 + n_add:]
    x = alpha * h_ref[...]
    for r in add_refs:
        x = x + r[...].astype(F32)
    mu = jnp.mean(x, axis=-1, keepdims=True)
    xc = x - mu
    var = jnp.mean(xc * xc, axis=-1, keepdims=True)
    y = xc * lax.rsqrt(var + LN_EPS) * g_ref[...] + b_ref[...]
    o_ref[...] = y
    ob_ref[...] = y.astype(ob_ref.dtype)


def _ln(h, addends, g, b, alpha):
    n, d = h.shape
    tm = min(256, n)
    row = pl.BlockSpec((tm, d), lambda i: (i, 0))
    vec = pl.BlockSpec((1, d), lambda i: (0, 0))
    return pl.pallas_call(
        functools.partial(_ln_kernel, alpha=alpha, n_add=len(addends)),
        grid=(n // tm,),
        in_specs=[row] * (1 + len(addends)) + [vec, vec],
        out_specs=[row, row],
        out_shape=[jax.ShapeDtypeStruct((n, d), F32), jax.ShapeDtypeStruct((n, d), BF16)],
        compiler_params=_cparams(1),
        name="deepnorm_ln",
    )(h, *addends, g, b)


def _topk_rows(x_ref, val_ref, idx_ref, rank_ref, k):
    r, t = x_ref.shape
    iota = lax.broadcasted_iota(jnp.int32, (r, t), 0).astype(F32)
    if rank_ref is not None:
        rank_ref[...] = jnp.full((r, t), float(k), F32)

    def body(kk, carry):
        x = x_ref[...]
        m = jnp.max(x, axis=0, keepdims=True)
        idx = jnp.min(jnp.where(x == m, iota, float(r)), axis=0, keepdims=True)
        sel = iota == idx
        x_ref[...] = jnp.where(sel, -jnp.inf, x)
        if rank_ref is not None:
            rank_ref[...] = jnp.where(sel, kk.astype(F32), rank_ref[...])
        val_ref[pl.ds(kk, 1), :] = m
        idx_ref[pl.ds(kk, 1), :] = idx
        return carry

    lax.fori_loop(0, k, body, 0)


def _route_kernel(qp_ref, k1_ref, k2_ref, nofi_ref, c_ref, r2_ref, e2_ref,
                  s_sc, cand_sc, t1_sc, i1_sc, t2_sc, i2_sc, top_sc, pos_sc, r1_sc):
    kk = PEER_TOPK
    qp = qp_ref[...].astype(BF16)
    nt = (((1,), (1,)), ((), ()))
    s1 = lax.dot_general(k1_ref[0].astype(BF16), qp[:, :PEER_HALF], nt, preferred_element_type=F32)
    s2 = lax.dot_general(k2_ref[0].astype(BF16), qp[:, PEER_HALF:], nt, preferred_element_type=F32)

    s_sc[...] = s1
    _topk_rows(s_sc, t1_sc, i1_sc, r1_sc, kk)
    s_sc[...] = s2
    _topk_rows(s_sc, t2_sc, i2_sc, r2_ref.at[0], kk)

    t2 = t2_sc[...]
    for a in range(kk):
        cand_sc[a * kk:(a + 1) * kk, :] = t1_sc[a:a + 1, :] + t2
    _topk_rows(cand_sc, top_sc, pos_sc, None, kk)

    top = top_sc[...]
    z = jnp.sum(jnp.exp(top - top[0:1, :]), axis=0, keepdims=True)
    a_sel = jnp.floor(pos_sc[...] * (1.0 / kk))
    r1 = r1_sc[...]
    nofi = jnp.zeros_like(r1)
    for a in range(kk):
        n_a = jnp.sum((a_sel == float(a)).astype(F32), axis=0, keepdims=True)
        nofi = nofi + jnp.where(r1 == float(a), n_a, 0.0)
    nofi_ref[0] = nofi
    c_ref[0] = jnp.exp(s1 - t1_sc[0:1, :]) / z
    e2_ref[0] = jnp.exp(s2 - t2_sc[0:1, :])


def _route(qp, sub_k1, sub_k2):
    n = qp.shape[0]
    tr = min(512, n)
    kk = PEER_TOPK
    out = jax.ShapeDtypeStruct((PEER_HEADS, PEER_KEYS, n), F32)
    ospec = pl.BlockSpec((1, PEER_KEYS, tr), lambda i, h: (h, 0, i))
    kspec = pl.BlockSpec((1, PEER_KEYS, PEER_HALF), lambda i, h: (h, 0, 0))
    small = pltpu.VMEM((kk, tr), F32)
    return pl.pallas_call(
        _route_kernel,
        grid=(n // tr, PEER_HEADS),
        in_specs=[pl.BlockSpec((tr, 2 * PEER_HALF), lambda i, h: (i, h)), kspec, kspec],
        out_specs=[ospec] * 4,
        out_shape=[out] * 4,
        scratch_shapes=[pltpu.VMEM((PEER_KEYS, tr), F32), pltpu.VMEM((kk * kk, tr), F32),
                        small, small, small, small, small, small,
                        pltpu.VMEM((PEER_KEYS, tr), F32)],
        compiler_params=_cparams(2),
        name="peer_route",
    )(qp, sub_k1, sub_k2)


def _gelu_tanh(x):
    c = math.sqrt(2.0 / math.pi)
    return 0.5 * x * (1.0 + jnp.tanh(c * (x + 0.044715 * (x * x * x))))


def _expert_kernel(u_ref, vt_ref, h_ref, nofi_ref, c_ref, r2_ref, e2_ref, o_ref, acc_sc, *, te):
    e = pl.program_id(1)

    @pl.when(e == 0)
    def _():
        acc_sc[...] = jnp.zeros_like(acc_sc)

    act = lax.dot_general(u_ref[...], h_ref[...], (((1,), (1,)), ((), ())),
                          preferred_element_type=F32)
    g = _gelu_tanh(act)
    rows = []
    for il in range(te // PEER_KEYS):
        i = e * (te // PEER_KEYS) + il
        w = None
        for h in range(PEER_HEADS):
            n_i = nofi_ref[h, pl.ds(i, 1), :]
            c_i = c_ref[h, pl.ds(i, 1), :]
            term = jnp.where(r2_ref[h] < n_i, e2_ref[h] * c_i, 0.0)
            w = term if w is None else w + term
        rows.append((g[il * PEER_KEYS:(il + 1) * PEER_KEYS, :] * w).astype(BF16))
    gt = rows[0] if len(rows) == 1 else jnp.concatenate(rows, axis=0)
    acc_sc[...] += jnp.dot(vt_ref[...], gt, preferred_element_type=F32)

    @pl.when(e == pl.num_programs(1) - 1)
    def _():
        o_ref[...] = acc_sc[...].T.astype(o_ref.dtype)


def _experts(u_b, vt_b, h_b, nofi, c, r2, e2):
    n, d = h_b.shape
    ne = u_b.shape[0]
    tm = min(512, n)
    te = 256
    rspec = pl.BlockSpec((PEER_HEADS, PEER_KEYS, tm), lambda b, e: (0, 0, b))
    return pl.pallas_call(
        functools.partial(_expert_kernel, te=te),
        grid=(n // tm, ne // te),
        in_specs=[pl.BlockSpec((te, d), lambda b, e: (e, 0)),
                  pl.BlockSpec((d, te), lambda b, e: (0, e)),
                  pl.BlockSpec((tm, d), lambda b, e: (b, 0)),
                  rspec, rspec, rspec, rspec],
        out_specs=pl.BlockSpec((tm, d), lambda b, e: (b, 0)),
        out_shape=jax.ShapeDtypeStruct((n, d), BF16),
        scratch_shapes=[pltpu.VMEM((d, tm), F32)],
        compiler_params=_cparams(2),
        name="peer_experts",
    )(u_b, vt_b, h_b, nofi, c, r2, e2)


def _ple_kernel(h_ref, wg_ref, p_ref, wp_ref, o_ref):
    g = jnp.dot(h_ref[...], wg_ref[...], preferred_element_type=F32)
    pp = jnp.dot(p_ref[...], wp_ref[...], preferred_element_type=F32)
    o_ref[...] = (jax.nn.sigmoid(g) * pp).astype(o_ref.dtype)


def _ple(h_b, w_pg, p_b, w_pe):
    n, d = h_b.shape
    kp = p_b.shape[1]
    tm, tn = min(1024, n), 1024
    return pl.pallas_call(
        _ple_kernel,
        grid=(n // tm, d // tn),
        in_specs=[pl.BlockSpec((tm, d), lambda i, j: (i, 0)),
                  pl.BlockSpec((d, tn), lambda i, j: (0, j)),
                  pl.BlockSpec((tm, kp), lambda i, j: (i, 0)),
                  pl.BlockSpec((kp, tn), lambda i, j: (0, j))],
        out_specs=pl.BlockSpec((tm, tn), lambda i, j: (i, j)),
        out_shape=jax.ShapeDtypeStruct((n, d), BF16),
        compiler_params=_cparams(2),
        name="ple",
    )(h_b, w_pg, p_b, w_pe)


def _layer_weights(w_in, w_uq, w_ukv, b_f):
    d = w_in.shape[0]
    sizes = (MLA_Q_RANK, MLA_KV_RANK, MLA_ROPE, FOX_HEADS * FOX_DIM, FOX_HEADS * FOX_DIM,
             FOX_HEADS * FOX_DIM, FOX_HEADS, d, d)
    offs = np.concatenate([[0], np.cumsum(sizes)])
    cq, ckv, kr, fq, fk, fv, fl, ga, gb = [w_in[:, offs[t]:offs[t + 1]] for t in range(9)]
    zpad = lambda c: jnp.zeros((d, c), w_in.dtype)
    w_small = jnp.concatenate([cq, ckv, kr, zpad(LANES - MLA_ROPE), fl, zpad(LANES - FOX_HEADS)],
                              axis=1).astype(BF16)
    w_big = jnp.concatenate([fq, fk, fv, ga, gb], axis=1).astype(BF16)
    nbig = w_big.shape[1]
    colscale = np.ones((1, nbig), np.float32)
    colscale[0, :FOX_HEADS * FOX_DIM] = FOX_DIM ** -0.5
    qk = MLA_NOPE + MLA_ROPE
    wq3 = w_uq.reshape(MLA_Q_RANK, MLA_HEADS, qk)
    wqn = wq3[:, :, :MLA_NOPE].reshape(MLA_Q_RANK, MLA_HEADS * MLA_NOPE).astype(BF16)
    wqr = jnp.pad(wq3[:, :, MLA_NOPE:], ((0, 0), (0, 0), (0, LANES - MLA_ROPE))
                  ).reshape(MLA_Q_RANK, MLA_HEADS * LANES).astype(BF16)
    wkv3 = w_ukv.reshape(MLA_KV_RANK, MLA_HEADS, MLA_NOPE + MLA_V)
    wuk = wkv3[:, :, :MLA_NOPE].reshape(MLA_KV_RANK, MLA_HEADS * MLA_NOPE).astype(BF16)
    wuv = wkv3[:, :, MLA_NOPE:].reshape(MLA_KV_RANK, MLA_HEADS * MLA_V).astype(BF16)
    bf_row = jnp.pad(b_f, (0, LANES - FOX_HEADS)).reshape(1, LANES)
    return w_small, w_big, jnp.asarray(colscale), wqn, wqr, wuk, wuv, bf_row


def kernel(x, p, positions, w_in, g_cq, g_ckv, w_uq, w_ukv, b_f, w_ya, w_yb, w_o, ln1_g, ln1_b,
           w_pq, sub_k1, sub_k2, u_tab, v_tab, w_pg, w_pe, ln2_g, ln2_b):
    bsz, seq, d = x.shape
    depth = w_in.shape[0]
    n = bsz * seq
    assert bsz == 1, "token-major layout assumes a single sequence"
    alpha = (2 * depth) ** 0.25
    fox_w = FOX_HEADS * FOX_DIM

    h = x.reshape(n, d)
    h_b = h.astype(BF16)
    tabs = _rope_tables(positions.reshape(n, 1))

    for i in range(depth):
        w_small, w_big, colscale, wqn, wqr, wuk, wuv, bf_row = _layer_weights(
            w_in[i], w_uq[i], w_ukv[i], b_f[i])

        zs = _mm(h_b, w_small, F32, 1024, 256, name="in_proj_small")
        zb = _mm(h_b, w_big, BF16, 1024, 1024, colscale, name="in_proj_big")
        q_mla = _mla_q(zs, g_cq[i].reshape(1, -1), wqn, wqr, tabs)
        k_mla, v_mla = _mla_kv(zs, g_ckv[i].reshape(1, -1), wuk, wuv, tabs)
        y_a = _attention(q_mla, k_mla, v_mla, MLA_HEADS, 2 * LANES, MLA_V, 0, 0, 0)
        ck = _fox_cum(zs, bf_row).reshape(FOX_HEADS, 1, n)
        y_b = _attention(zb, zb, zb, FOX_HEADS, FOX_DIM, FOX_DIM, 0, FOX_HEADS, 2 * FOX_HEADS, ck)
        merged = _merge(y_a, y_b, w_ya[i].astype(BF16), w_yb[i].astype(BF16), zb,
                        3 * fox_w, 3 * fox_w + d)
        mix = _mm(merged, w_o[i].astype(BF16), F32, 1024, 1024, name="out_proj")
        h, h_b = _ln(h, [mix], ln1_g[i].reshape(1, d), ln1_b[i].reshape(1, d), alpha)

        qp = _mm(h_b, w_pq[i].astype(BF16), F32, 1024, 1024, name="peer_query")
        nofi, c, r2, e2 = _route(qp, sub_k1[i], sub_k2[i])
        ffn = _experts(u_tab[i].astype(BF16), v_tab[i].T.astype(BF16), h_b, nofi, c, r2, e2)
        ple = _ple(h_b, w_pg[i].astype(BF16), p[i].reshape(n, -1).astype(BF16), w_pe[i].astype(BF16))
        h, h_b = _ln(h, [ffn, ple], ln2_g[i].reshape(1, d), ln2_b[i].reshape(1, d), alpha)

    return h.reshape(bsz, seq, d)
```

```python
import functools
import math

import numpy as np
import jax
import jax.numpy as jnp
from jax import lax
from jax.experimental import pallas as pl
from jax.experimental.pallas import tpu as pltpu

F32 = jnp.float32
BF16 = jnp.bfloat16

MLA_HEADS = 16
MLA_Q_RANK = 1024
MLA_KV_RANK = 512
MLA_NOPE = 128
MLA_ROPE = 64
MLA_V = 128
ROPE_THETA = 10000.0
FOX_HEADS = 16
FOX_DIM = 128
PEER_HEADS = 8
PEER_KEYS = 128
PEER_HALF = 128
PEER_TOPK = 16
LN_EPS = 1e-5
RMS_EPS = 1e-6
LANES = 128
BF16_SUBLANES = 16
LOG2E = math.log2(math.e)
NEG = -1e30
ATTN_CHUNK = 512
ATTN_SUBBLOCKS = 2

VMEM_LIMIT = 56 * 1024 * 1024


def _cparams(n_axes):
    return pltpu.CompilerParams(dimension_semantics=("arbitrary",) * n_axes,
                                vmem_limit_bytes=VMEM_LIMIT)


def _mm_kernel(x_ref, w_ref, o_ref):
    o_ref[...] = jnp.dot(x_ref[...], w_ref[...], preferred_element_type=F32).astype(o_ref.dtype)


def _mm_scale_kernel(x_ref, w_ref, s_ref, o_ref):
    acc = jnp.dot(x_ref[...], w_ref[...], preferred_element_type=F32)
    o_ref[...] = (acc * s_ref[...]).astype(o_ref.dtype)


def _mm(x, w, out_dtype, tm, tn, colscale=None, name="mm"):
    m, k = x.shape
    _, n = w.shape
    tm, tn = min(tm, m), min(tn, n)
    assert m % tm == 0 and n % tn == 0
    in_specs = [pl.BlockSpec((tm, k), lambda i, j: (i, 0)),
                pl.BlockSpec((k, tn), lambda i, j: (0, j))]
    args = [x, w]
    body = _mm_kernel
    if colscale is not None:
        in_specs.append(pl.BlockSpec((1, tn), lambda i, j: (0, j)))
        args.append(colscale)
        body = _mm_scale_kernel
    return pl.pallas_call(
        body,
        grid=(m // tm, n // tn),
        in_specs=in_specs,
        out_specs=pl.BlockSpec((tm, tn), lambda i, j: (i, j)),
        out_shape=jax.ShapeDtypeStruct((m, n), out_dtype),
        compiler_params=_cparams(2),
        name=name,
    )(*args)


def _rope_table_kernel(pos_ref, c_ref, cos_ref, sina_ref, sinb_ref):
    ang = pos_ref[...].astype(F32) * c_ref[0:1, :]
    cos = jnp.cos(ang)
    sin = jnp.sin(ang)
    cos_ref[...] = cos * c_ref[1:2, :]
    sina_ref[...] = sin * c_ref[2:3, :]
    sinb_ref[...] = sin * c_ref[3:4, :]


def _rope_tables(positions_col):
    n = positions_col.shape[0]
    tm = min(1024, n)
    half = MLA_ROPE // 2
    inv_freq = (ROPE_THETA ** (-np.arange(half, dtype=np.float32) / half)).astype(np.float32)
    consts = np.zeros((8, LANES), np.float32)
    consts[0, :half] = inv_freq
    consts[0, half:2 * half] = inv_freq
    consts[1, :2 * half] = 1.0
    consts[2, :half] = -1.0
    consts[3, half:2 * half] = 1.0
    shp = jax.ShapeDtypeStruct((n, LANES), F32)
    return pl.pallas_call(
        _rope_table_kernel,
        grid=(n // tm,),
        in_specs=[pl.BlockSpec((tm, 1), lambda i: (i, 0)),
                  pl.BlockSpec((8, LANES), lambda i: (0, 0))],
        out_specs=[pl.BlockSpec((tm, LANES), lambda i: (i, 0))] * 3,
        out_shape=[shp, shp, shp],
        compiler_params=_cparams(1),
        name="rope_tables",
    )(positions_col, jnp.asarray(consts))


def _rope128(x, cos, sina, sinb):
    return (x * cos + pltpu.roll(x, 3 * (MLA_ROPE // 2), 1) * sina
            + pltpu.roll(x, MLA_ROPE // 2, 1) * sinb)


def _rms(x, g):
    return x * lax.rsqrt(jnp.mean(x * x, axis=-1, keepdims=True) + RMS_EPS) * g


def _mla_q_kernel(cq_ref, g_ref, wn_ref, wr_ref, cos_ref, sina_ref, sinb_ref, q_ref, *, scale):
    xn = _rms(cq_ref[...], g_ref[...]).astype(BF16)
    qn = jnp.dot(xn, wn_ref[...], preferred_element_type=F32) * scale
    qr = jnp.dot(xn, wr_ref[...], preferred_element_type=F32) * scale
    cos, sina, sinb = cos_ref[...], sina_ref[...], sinb_ref[...]
    for hh in range(2):
        lo = hh * LANES
        q_ref[:, 2 * lo:2 * lo + LANES] = qn[:, lo:lo + LANES].astype(q_ref.dtype)
        q_ref[:, 2 * lo + LANES:2 * lo + 2 * LANES] = _rope128(
            qr[:, lo:lo + LANES], cos, sina, sinb).astype(q_ref.dtype)


def _mla_q(zs, g_cq, wqn, wqr, tabs):
    n = zs.shape[0]
    tm = min(512, n)
    pairs = MLA_HEADS // 2
    scale = (MLA_NOPE + MLA_ROPE) ** -0.5 * LOG2E
    tab_spec = pl.BlockSpec((tm, LANES), lambda i, p: (i, 0))
    return pl.pallas_call(
        functools.partial(_mla_q_kernel, scale=scale),
        grid=(n // tm, pairs),
        in_specs=[pl.BlockSpec((tm, MLA_Q_RANK), lambda i, p: (i, 0)),
                  pl.BlockSpec((1, MLA_Q_RANK), lambda i, p: (0, 0)),
                  pl.BlockSpec((MLA_Q_RANK, 2 * LANES), lambda i, p: (0, p)),
                  pl.BlockSpec((MLA_Q_RANK, 2 * LANES), lambda i, p: (0, p)),
                  tab_spec, tab_spec, tab_spec],
        out_specs=pl.BlockSpec((tm, 4 * LANES), lambda i, p: (i, p)),
        out_shape=jax.ShapeDtypeStruct((n, MLA_HEADS * 2 * LANES), BF16),
        compiler_params=_cparams(2),
        name="mla_q",
    )(zs, g_cq, wqn, wqr, *tabs)


def _mla_kv_kernel(ckv_ref, g_ref, kr_ref, wk_ref, wv_ref, cos_ref, sina_ref, sinb_ref,
                   k_ref, v_ref):
    xn = _rms(ckv_ref[...], g_ref[...]).astype(BF16)
    kn = jnp.dot(xn, wk_ref[...], preferred_element_type=F32)
    v_ref[...] = jnp.dot(xn, wv_ref[...], preferred_element_type=F32).astype(v_ref.dtype)
    kr = _rope128(kr_ref[...], cos_ref[...], sina_ref[...], sinb_ref[...]).astype(k_ref.dtype)
    for hh in range(2):
        lo = hh * LANES
        k_ref[:, 2 * lo:2 * lo + LANES] = kn[:, lo:lo + LANES].astype(k_ref.dtype)
        k_ref[:, 2 * lo + LANES:2 * lo + 2 * LANES] = kr


def _mla_kv(zs, g_ckv, wuk, wuv, tabs):
    n = zs.shape[0]
    tm = min(512, n)
    pairs = MLA_HEADS // 2
    ckv_blk = MLA_Q_RANK // MLA_KV_RANK
    kr_blk = (MLA_Q_RANK + MLA_KV_RANK) // LANES
    tab_spec = pl.BlockSpec((tm, LANES), lambda i, p: (i, 0))
    return pl.pallas_call(
        _mla_kv_kernel,
        grid=(n // tm, pairs),
        in_specs=[pl.BlockSpec((tm, MLA_KV_RANK), lambda i, p: (i, ckv_blk)),
                  pl.BlockSpec((1, MLA_KV_RANK), lambda i, p: (0, 0)),
                  pl.BlockSpec((tm, LANES), lambda i, p: (i, kr_blk)),
                  pl.BlockSpec((MLA_KV_RANK, 2 * LANES), lambda i, p: (0, p)),
                  pl.BlockSpec((MLA_KV_RANK, 2 * LANES), lambda i, p: (0, p)),
                  tab_spec, tab_spec, tab_spec],
        out_specs=[pl.BlockSpec((tm, 4 * LANES), lambda i, p: (i, p)),
                   pl.BlockSpec((tm, 2 * LANES), lambda i, p: (i, p))],
        out_shape=[jax.ShapeDtypeStruct((n, MLA_HEADS * 2 * LANES), BF16),
                   jax.ShapeDtypeStruct((n, MLA_HEADS * MLA_V), BF16)],
        compiler_params=_cparams(2),
        name="mla_kv",
    )(zs, g_ckv, zs, wuk, wuv, *tabs)


def _split3(x):
    hi = x.astype(BF16)
    r = x - hi.astype(F32)
    mid = r.astype(BF16)
    lo = (r - mid.astype(F32)).astype(BF16)
    return hi, mid, lo


def _cum_kernel(fl_ref, bf_ref, tri_ref, o_ref, carry_ref):
    @pl.when(pl.program_id(0) == 0)
    def _():
        carry_ref[...] = jnp.zeros_like(carry_ref)

    x = fl_ref[...] + bf_ref[...]
    lf = jnp.minimum(x, 0.0) - jnp.log1p(jnp.exp(-jnp.abs(x)))
    tri = tri_ref[...]
    c = carry_ref[...]
    for part in _split3(lf):
        c = c + jnp.dot(tri, part, preferred_element_type=F32)
    carry_ref[...] = c[c.shape[0] - 1:, :]
    o_ref[...] = c


def _fox_cum(zs, bf_row):
    n = zs.shape[0]
    tm = 256
    fl_blk = (MLA_Q_RANK + MLA_KV_RANK) // LANES + 1
    tri = jnp.asarray(np.tril(np.ones((tm, tm), np.float32)), dtype=BF16)
    return pl.pallas_call(
        _cum_kernel,
        grid=(n // tm,),
        in_specs=[pl.BlockSpec((tm, LANES), lambda i: (i, fl_blk)),
                  pl.BlockSpec((1, LANES), lambda i: (0, 0)),
                  pl.BlockSpec((tm, tm), lambda i: (0, 0))],
        out_specs=pl.BlockSpec((tm, LANES), lambda i: (i, 0)),
        out_shape=jax.ShapeDtypeStruct((n, LANES), F32),
        scratch_shapes=[pltpu.VMEM((1, LANES), F32)],
        compiler_params=_cparams(1),
        name="fox_cum",
    )(zs, bf_row, tri)


def _fox_pack_kernel(fq_ref, fk_ref, cum_ref, e_ref, q_ref, k_ref):
    c3 = jnp.concatenate(_split3(cum_ref[...] * LOG2E), axis=1)
    kx = jnp.dot(c3, e_ref[0], preferred_element_type=F32)
    lane = lax.broadcasted_iota(jnp.int32, (fq_ref.shape[0], LANES), 1)
    qaug = jnp.where(lane < 3, -1.0, 0.0).astype(q_ref.dtype)
    for hh in range(2):
        lo = hh * LANES
        q_ref[:, 2 * lo:2 * lo + LANES] = fq_ref[:, lo:lo + LANES]
        q_ref[:, 2 * lo + LANES:2 * lo + 2 * LANES] = qaug
        k_ref[:, 2 * lo:2 * lo + LANES] = fk_ref[:, lo:lo + LANES]
        k_ref[:, 2 * lo + LANES:2 * lo + 2 * LANES] = kx[:, 2 * lo + LANES:2 * lo + 2 * LANES].astype(k_ref.dtype)


def _fox_pack(zb, cum):
    n = zb.shape[0]
    tm = min(1024, n)
    pairs = FOX_HEADS // 2
    place = np.zeros((pairs, 3 * LANES, 4 * LANES), np.float32)
    for p in range(pairs):
        for hh in range(2):
            for part in range(3):
                place[p, part * LANES + 2 * p + hh, hh * 2 * LANES + LANES + part] = 1.0
    out = jax.ShapeDtypeStruct((n, FOX_HEADS * 2 * LANES), BF16)
    ospec = pl.BlockSpec((tm, 4 * LANES), lambda i, p: (i, p))
    return pl.pallas_call(
        _fox_pack_kernel,
        grid=(n // tm, pairs),
        in_specs=[pl.BlockSpec((tm, 2 * LANES), lambda i, p: (i, p)),
                  pl.BlockSpec((tm, 2 * LANES), lambda i, p: (i, pairs + p)),
                  pl.BlockSpec((tm, LANES), lambda i, p: (i, 0)),
                  pl.BlockSpec((1, 3 * LANES, 4 * LANES), lambda i, p: (p, 0, 0))],
        out_specs=[ospec, ospec],
        out_shape=[out, out],
        compiler_params=_cparams(2),
        name="fox_pack",
    )(zb, zb, cum, jnp.asarray(place, dtype=BF16))


def _attn_kernel(q_ref, k_ref, v_ref, o_ref, vt_sc, s_sc, m_sc, acc_sc, *, tc, nsub, dv):
    i = pl.program_id(1)
    nchunks, dva, _ = vt_sc.shape

    @pl.when(i == 0)
    def _():
        ones_row = (lax.broadcasted_iota(jnp.int32, (dva - dv, tc), 0) == 0).astype(vt_sc.dtype)

        def fill(c, carry):
            off = pl.multiple_of(c * tc, tc)
            vt_sc[c, 0:dv, :] = v_ref[pl.ds(off, tc), :].astype(F32).T.astype(vt_sc.dtype)
            vt_sc[c, dv:dva, :] = ones_row
            return carry

        lax.fori_loop(0, nchunks, fill, 0)

    m_sc[...] = jnp.full_like(m_sc, NEG)
    acc_sc[...] = jnp.zeros_like(acc_sc)

    def scores(j, rs, slot):
        off = pl.multiple_of(j * tc, tc)
        k = k_ref[pl.ds(off, tc), :]
        for r in rs:
            s_sc[slot, r] = lax.dot_general(k, q_ref[r * tc:(r + 1) * tc, :],
                                            (((1,), (1,)), ((), ())), preferred_element_type=F32)

    def update(j, rs, slot, masked_r):
        for r in rs:
            s = s_sc[slot, r]
            if r == masked_r:
                key = lax.broadcasted_iota(jnp.int32, (tc, tc), 0)
                qry = lax.broadcasted_iota(jnp.int32, (tc, tc), 1)
                s = jnp.where(key <= qry, s, NEG)
            m_prev = m_sc[r]
            m_new = jnp.maximum(m_prev, jnp.max(s, axis=0, keepdims=True))
            alpha = jnp.exp2(m_prev - m_new)
            p = jnp.exp2(s - m_new).astype(vt_sc.dtype)
            acc_sc[r] = alpha * acc_sc[r] + jnp.dot(vt_sc[j], p, preferred_element_type=F32)
            m_sc[r] = m_new

    all_r = range(nsub)
    n_main = i * nsub
    scores(0, all_r, 0)

    def body(t, carry):
        j = 2 * t
        scores(j + 1, all_r, 1)
        update(j, all_r, 0, -1)
        scores(j + 2, all_r, 0)
        update(j + 1, all_r, 1, -1)
        return carry

    lax.fori_loop(0, n_main // 2, body, 0)
    for jj in range(nsub):
        if jj + 1 < nsub:
            scores(n_main + jj + 1, range(jj + 1, nsub), (jj + 1) % 2)
        update(n_main + jj, range(jj, nsub), jj % 2, jj)
    for r in range(nsub):
        acc = acc_sc[r]
        o = acc[0:dv, :] / acc[dv:dv + 1, :]
        o_ref[r * tc:(r + 1) * tc, :] = o.T.astype(o_ref.dtype)


def _attention(q_arr, k_arr, v_arr, heads, dk, dv, q_blk0, k_blk0, v_blk0, name):
    n = q_arr.shape[0]
    tc = min(ATTN_CHUNK, n)
    nsub = min(ATTN_SUBBLOCKS, n // tc)
    tq = tc * nsub
    assert nsub % 2 == 0 and n % tq == 0
    dva = dv + BF16_SUBLANES
    return pl.pallas_call(
        functools.partial(_attn_kernel, tc=tc, nsub=nsub, dv=dv),
        grid=(heads, n // tq),
        in_specs=[pl.BlockSpec((tq, dk), lambda h, i: (i, q_blk0 + h)),
                  pl.BlockSpec((n, dk), lambda h, i: (0, k_blk0 + h)),
                  pl.BlockSpec((n, dv), lambda h, i: (0, v_blk0 + h))],
        out_specs=pl.BlockSpec((tq, dv), lambda h, i: (i, h)),
        out_shape=jax.ShapeDtypeStruct((n, heads * dv), BF16),
        scratch_shapes=[pltpu.VMEM((n // tc, dva, tc), BF16),
                        pltpu.VMEM((2, nsub, tc, tc), F32),
                        pltpu.VMEM((nsub, 1, tc), F32),
                        pltpu.VMEM((nsub, dva, tc), F32)],
        compiler_params=_cparams(2),
        name=name,
    )(q_arr, k_arr, v_arr)


def _merge_kernel(ya_ref, yb_ref, wa_ref, wb_ref, ga_ref, gb_ref, o_ref):
    a = jnp.dot(ya_ref[...], wa_ref[...], preferred_element_type=F32)
    b = jnp.dot(yb_ref[...], wb_ref[...], preferred_element_type=F32)
    ga = jax.nn.sigmoid(ga_ref[...].astype(F32))
    gb = jax.nn.sigmoid(gb_ref[...].astype(F32))
    o_ref[...] = (ga * a + gb * b).astype(o_ref.dtype)


def _merge(ya, yb, w_ya, w_yb, zb, ga_col0, gb_col0):
    n, ka = ya.shape
    kb = yb.shape[1]
    d = w_ya.shape[1]
    tm, tn = min(1024, n), 512
    ga0, gb0 = ga_col0 // tn, gb_col0 // tn
    return pl.pallas_call(
        _merge_kernel,
        grid=(n // tm, d // tn),
        in_specs=[pl.BlockSpec((tm, ka), lambda i, j: (i, 0)),
                  pl.BlockSpec((tm, kb), lambda i, j: (i, 0)),
                  pl.BlockSpec((ka, tn), lambda i, j: (0, j)),
                  pl.BlockSpec((kb, tn), lambda i, j: (0, j)),
                  pl.BlockSpec((tm, tn), lambda i, j: (i, ga0 + j)),
                  pl.BlockSpec((tm, tn), lambda i, j: (i, gb0 + j))],
        out_specs=pl.BlockSpec((tm, tn), lambda i, j: (i, j)),
        out_shape=jax.ShapeDtypeStruct((n, d), BF16),
        compiler_params=_cparams(2),
        name="merge",
    )(ya, yb, w_ya, w_yb, zb, zb)


def _ln_kernel(*refs, alpha, n_add, with_t):
    h_ref = refs[0]
    add_refs = refs[1:1 + n_add]
    g_ref, b_ref, o_ref, ob_ref = refs[1 + n_add:5 + n_add]
    x = alpha * h_ref[...]
    for r in add_refs:
        x = x + r[...].astype(F32)
    mu = jnp.mean(x, axis=-1, keepdims=True)
    xc = x - mu
    var = jnp.mean(xc * xc, axis=-1, keepdims=True)
    y = xc * lax.rsqrt(var + LN_EPS) * g_ref[...] + b_ref[...]
    o_ref[...] = y
    ob_ref[...] = y.astype(ob_ref.dtype)
    if with_t:
        obt_ref = refs[5 + n_add]
        obt_ref[...] = y.T.astype(obt_ref.dtype)


def _ln(h, addends, g, b, alpha, with_t):
    n, d = h.shape
    tm = min(256, n)
    row = pl.BlockSpec((tm, d), lambda i: (i, 0))
    vec = pl.BlockSpec((1, d), lambda i: (0, 0))
    out_specs = [row, row]
    out_shape = [jax.ShapeDtypeStruct((n, d), F32), jax.ShapeDtypeStruct((n, d), BF16)]
    if with_t:
        out_specs.append(pl.BlockSpec((d, tm), lambda i: (0, i)))
        out_shape.append(jax.ShapeDtypeStruct((d, n), BF16))
    return pl.pallas_call(
        functools.partial(_ln_kernel, alpha=alpha, n_add=len(addends), with_t=with_t),
        grid=(n // tm,),
        in_specs=[row] * (1 + len(addends)) + [vec, vec],
        out_specs=out_specs,
        out_shape=out_shape,
        compiler_params=_cparams(1),
        name="deepnorm_ln",
    )(h, *addends, g, b)


def _topk_rows(x_ref, val_ref, idx_ref, rank_ref, k):
    r, t = x_ref.shape
    iota = lax.broadcasted_iota(jnp.int32, (r, t), 0).astype(F32)
    if rank_ref is not None:
        rank_ref[...] = jnp.full((r, t), float(k), F32)

    def body(kk, carry):
        x = x_ref[...]
        m = jnp.max(x, axis=0, keepdims=True)
        idx = jnp.min(jnp.where(x == m, iota, float(r)), axis=0, keepdims=True)
        sel = iota == idx
        x_ref[...] = jnp.where(sel, -jnp.inf, x)
        if rank_ref is not None:
            rank_ref[...] = jnp.where(sel, kk.astype(F32), rank_ref[...])
        val_ref[pl.ds(kk, 1), :] = m
        idx_ref[pl.ds(kk, 1), :] = idx
        return carry

    lax.fori_loop(0, k, body, 0)


def _route_kernel(qp_ref, k1_ref, k2_ref, nofi_ref, c_ref, r2_ref, e2_ref,
                  s_sc, cand_sc, t1_sc, i1_sc, t2_sc, i2_sc, top_sc, pos_sc, r1_sc, r2_sc):
    kk = PEER_TOPK
    qp = qp_ref[...].astype(BF16)
    nt = (((1,), (1,)), ((), ()))
    s1 = lax.dot_general(k1_ref[0].astype(BF16), qp[:, :PEER_HALF], nt, preferred_element_type=F32)
    s2 = lax.dot_general(k2_ref[0].astype(BF16), qp[:, PEER_HALF:], nt, preferred_element_type=F32)

    s_sc[...] = s1
    _topk_rows(s_sc, t1_sc, i1_sc, r1_sc, kk)
    s_sc[...] = s2
    _topk_rows(s_sc, t2_sc, i2_sc, r2_sc, kk)

    t2 = t2_sc[...]
    for a in range(kk):
        cand_sc[a * kk:(a + 1) * kk, :] = t1_sc[a:a + 1, :] + t2
    _topk_rows(cand_sc, top_sc, pos_sc, None, kk)

    top = top_sc[...]
    z = jnp.sum(jnp.exp(top - top[0:1, :]), axis=0, keepdims=True)
    a_sel = jnp.floor(pos_sc[...] * (1.0 / kk))
    r1 = r1_sc[...]
    nofi = jnp.zeros_like(r1)
    for a in range(kk):
        n_a = jnp.sum((a_sel == float(a)).astype(F32), axis=0, keepdims=True)
        nofi = nofi + jnp.where(r1 == float(a), n_a, 0.0)
    nofi_ref[0] = nofi
    c_ref[0] = jnp.exp(s1 - t1_sc[0:1, :]) / z
    r2_ref[0] = r2_sc[...].astype(r2_ref.dtype)
    e2_ref[0] = jnp.exp(s2 - t2_sc[0:1, :]).astype(e2_ref.dtype)


def _route(qp, sub_k1, sub_k2):
    n = qp.shape[0]
    tr = min(512, n)
    kk = PEER_TOPK
    out = jax.ShapeDtypeStruct((PEER_HEADS, PEER_KEYS, n), F32)
    out_b = jax.ShapeDtypeStruct((PEER_HEADS, PEER_KEYS, n), BF16)
    ospec = pl.BlockSpec((1, PEER_KEYS, tr), lambda i, h: (h, 0, i))
    kspec = pl.BlockSpec((1, PEER_KEYS, PEER_HALF), lambda i, h: (h, 0, 0))
    small = pltpu.VMEM((kk, tr), F32)
    return pl.pallas_call(
        _route_kernel,
        grid=(n // tr, PEER_HEADS),
        in_specs=[pl.BlockSpec((tr, 2 * PEER_HALF), lambda i, h: (i, h)), kspec, kspec],
        out_specs=[ospec] * 4,
        out_shape=[out, out, out_b, out_b],
        scratch_shapes=[pltpu.VMEM((PEER_KEYS, tr), F32), pltpu.VMEM((kk * kk, tr), F32),
                        small, small, small, small, small, small,
                        pltpu.VMEM((PEER_KEYS, tr), F32), pltpu.VMEM((PEER_KEYS, tr), F32)],
        compiler_params=_cparams(2),
        name="peer_route",
    )(qp, sub_k1, sub_k2)


def _gelu_tanh(x):
    c = math.sqrt(2.0 / math.pi)
    return 0.5 * x * (1.0 + jnp.tanh(c * (x + 0.044715 * (x * x * x))))


def _gates(nofi_ref, c_ref, r2_ref, e2_ref, w_sc):
    for il in range(nofi_ref.shape[0]):
        w = None
        for h in range(PEER_HEADS):
            n_i = nofi_ref[il, h:h + 1, :].astype(BF16)
            c_i = c_ref[il, h:h + 1, :].astype(BF16)
            term = jnp.where(r2_ref[h] < n_i, e2_ref[h] * c_i, jnp.zeros((), BF16))
            w = term if w is None else w + term
        w_sc[il * PEER_KEYS:(il + 1) * PEER_KEYS, :] = w


def _expert_kernel(u_ref, vt_ref, ht_ref, nofi0_ref, c0_ref, nofi_ref, c_ref, r2_ref, e2_ref, o_ref,
                   acc_sc, g_sc, w_sc):
    e = pl.program_id(1)
    old = lax.rem(e + 1, 2)
    new = lax.rem(e, 2)

    @pl.when(e == 0)
    def _():
        acc_sc[...] = jnp.zeros_like(acc_sc)
        g_sc[...] = jnp.zeros_like(g_sc)
        _gates(nofi0_ref, c0_ref, r2_ref, e2_ref, w_sc.at[0])

    act = jnp.dot(u_ref[...], ht_ref[...], preferred_element_type=F32)
    acc_sc[...] += jnp.dot(vt_ref[...], g_sc[old], preferred_element_type=F32)
    g_sc[new] = _gelu_tanh(act).astype(g_sc.dtype) * w_sc[new]
    _gates(nofi_ref, c_ref, r2_ref, e2_ref, w_sc.at[old])

    @pl.when(e == pl.num_programs(1) - 1)
    def _():
        o_ref[...] = acc_sc[...].T.astype(o_ref.dtype)


def _experts(u_b, vt_b, ht_b, nofi_t, c_t, r2, e2):
    d, n = ht_b.shape
    ne = u_b.shape[0]
    tm = min(512, n)
    te = 512
    gi = te // PEER_KEYS
    nblk = ne // te
    cur = lambda e: jnp.minimum(e, nblk - 1)
    nxt = lambda e: jnp.minimum(e + 1, nblk - 1)
    prev = lambda e: jnp.maximum(e - 1, 0)
    ispec0 = pl.BlockSpec((gi, PEER_HEADS, tm), lambda b, e: (0, 0, b))
    ispec = pl.BlockSpec((gi, PEER_HEADS, tm), lambda b, e: (nxt(e), 0, b))
    jspec = pl.BlockSpec((PEER_HEADS, PEER_KEYS, tm), lambda b, e: (0, 0, b))
    return pl.pallas_call(
        _expert_kernel,
        grid=(n // tm, nblk + 1),
        in_specs=[pl.BlockSpec((te, d), lambda b, e: (cur(e), 0)),
                  pl.BlockSpec((d, te), lambda b, e: (0, prev(e))),
                  pl.BlockSpec((d, tm), lambda b, e: (0, b)),
                  ispec0, ispec0, ispec, ispec, jspec, jspec],
        out_specs=pl.BlockSpec((tm, d), lambda b, e: (b, 0)),
        out_shape=jax.ShapeDtypeStruct((n, d), BF16),
        scratch_shapes=[pltpu.VMEM((d, tm), F32), pltpu.VMEM((2, te, tm), BF16),
                        pltpu.VMEM((2, te, tm), BF16)],
        compiler_params=_cparams(2),
        name="peer_experts",
    )(u_b, vt_b, ht_b, nofi_t, c_t, nofi_t, c_t, r2, e2)


def _ple_kernel(h_ref, wg_ref, p_ref, wp_ref, o_ref):
    g = jnp.dot(h_ref[...], wg_ref[...], preferred_element_type=F32)
    pp = jnp.dot(p_ref[...], wp_ref[...], preferred_element_type=F32)
    o_ref[...] = (jax.nn.sigmoid(g) * pp).astype(o_ref.dtype)


def _ple(h_b, w_pg, p_b, w_pe):
    n, d = h_b.shape
    kp = p_b.shape[1]
    tm, tn = min(1024, n), 1024
    return pl.pallas_call(
        _ple_kernel,
        grid=(n // tm, d // tn),
        in_specs=[pl.BlockSpec((tm, d), lambda i, j: (i, 0)),
                  pl.BlockSpec((d, tn), lambda i, j: (0, j)),
                  pl.BlockSpec((tm, kp), lambda i, j: (i, 0)),
                  pl.BlockSpec((kp, tn), lambda i, j: (0, j))],
        out_specs=pl.BlockSpec((tm, tn), lambda i, j: (i, j)),
        out_shape=jax.ShapeDtypeStruct((n, d), BF16),
        compiler_params=_cparams(2),
        name="ple",
    )(h_b, w_pg, p_b, w_pe)


def _layer_weights(w_in, w_uq, w_ukv, b_f):
    d = w_in.shape[0]
    sizes = (MLA_Q_RANK, MLA_KV_RANK, MLA_ROPE, FOX_HEADS * FOX_DIM, FOX_HEADS * FOX_DIM,
             FOX_HEADS * FOX_DIM, FOX_HEADS, d, d)
    offs = np.concatenate([[0], np.cumsum(sizes)])
    cq, ckv, kr, fq, fk, fv, fl, ga, gb = [w_in[:, offs[t]:offs[t + 1]] for t in range(9)]
    zpad = lambda c: jnp.zeros((d, c), w_in.dtype)
    w_small = jnp.concatenate([cq, ckv, kr, zpad(LANES - MLA_ROPE), fl, zpad(LANES - FOX_HEADS)],
                              axis=1).astype(BF16)
    w_big = jnp.concatenate([fq, fk, fv, ga, gb], axis=1).astype(BF16)
    nbig = w_big.shape[1]
    colscale = np.ones((1, nbig), np.float32)
    colscale[0, :FOX_HEADS * FOX_DIM] = FOX_DIM ** -0.5 * LOG2E
    qk = MLA_NOPE + MLA_ROPE
    wq3 = w_uq.reshape(MLA_Q_RANK, MLA_HEADS, qk)
    wqn = wq3[:, :, :MLA_NOPE].reshape(MLA_Q_RANK, MLA_HEADS * MLA_NOPE).astype(BF16)
    wqr = jnp.pad(wq3[:, :, MLA_NOPE:], ((0, 0), (0, 0), (0, LANES - MLA_ROPE))
                  ).reshape(MLA_Q_RANK, MLA_HEADS * LANES).astype(BF16)
    wkv3 = w_ukv.reshape(MLA_KV_RANK, MLA_HEADS, MLA_NOPE + MLA_V)
    wuk = wkv3[:, :, :MLA_NOPE].reshape(MLA_KV_RANK, MLA_HEADS * MLA_NOPE).astype(BF16)
    wuv = wkv3[:, :, MLA_NOPE:].reshape(MLA_KV_RANK, MLA_HEADS * MLA_V).astype(BF16)
    bf_row = jnp.pad(b_f, (0, LANES - FOX_HEADS)).reshape(1, LANES)
    return w_small, w_big, jnp.asarray(colscale), wqn, wqr, wuk, wuv, bf_row


def kernel(x, p, positions, w_in, g_cq, g_ckv, w_uq, w_ukv, b_f, w_ya, w_yb, w_o, ln1_g, ln1_b,
           w_pq, sub_k1, sub_k2, u_tab, v_tab, w_pg, w_pe, ln2_g, ln2_b):
    bsz, seq, d = x.shape
    depth = w_in.shape[0]
    n = bsz * seq
    assert bsz == 1, "token-major layout assumes a single sequence"
    alpha = (2 * depth) ** 0.25
    fox_w = FOX_HEADS * FOX_DIM

    h = x.reshape(n, d)
    h_b = h.astype(BF16)
    tabs = _rope_tables(positions.reshape(n, 1))

    for i in range(depth):
        w_small, w_big, colscale, wqn, wqr, wuk, wuv, bf_row = _layer_weights(
            w_in[i], w_uq[i], w_ukv[i], b_f[i])

        zs = _mm(h_b, w_small, F32, 1024, 256, name="in_proj_small")
        zb = _mm(h_b, w_big, BF16, 1024, 1024, colscale, name="in_proj_big")
        q_mla = _mla_q(zs, g_cq[i].reshape(1, -1), wqn, wqr, tabs)
        k_mla, v_mla = _mla_kv(zs, g_ckv[i].reshape(1, -1), wuk, wuv, tabs)
        y_a = _attention(q_mla, k_mla, v_mla, MLA_HEADS, 2 * LANES, MLA_V, 0, 0, 0, "attn_mla")
        q_fox, k_fox = _fox_pack(zb, _fox_cum(zs, bf_row))
        y_b = _attention(q_fox, k_fox, zb, FOX_HEADS, 2 * LANES, FOX_DIM, 0, 0, 2 * FOX_HEADS,
                         "attn_fox")
        merged = _merge(y_a, y_b, w_ya[i].astype(BF16), w_yb[i].astype(BF16), zb,
                        3 * fox_w, 3 * fox_w + d)
        mix = _mm(merged, w_o[i].astype(BF16), F32, 1024, 1024, name="out_proj")
        h, h_b, ht_b = _ln(h, [mix], ln1_g[i].reshape(1, d), ln1_b[i].reshape(1, d), alpha, True)

        qp = _mm(h_b, w_pq[i].astype(BF16), F32, 1024, 1024, name="peer_query")
        nofi, c, r2, e2 = _route(qp, sub_k1[i], sub_k2[i])
        ffn = _experts(u_tab[i].astype(BF16), v_tab[i].T.astype(BF16), ht_b,
                       nofi.transpose(1, 0, 2), c.transpose(1, 0, 2), r2, e2)
        ple = _ple(h_b, w_pg[i].astype(BF16), p[i].reshape(n, -1).astype(BF16), w_pe[i].astype(BF16))
        h, h_b = _ln(h, [ffn, ple], ln2_g[i].reshape(1, d), ln2_b[i].reshape(1, d), alpha, False)

    return h.reshape(bsz, seq, d)
```

```python
import functools
import math

import numpy as np
import jax
import jax.numpy as jnp
from jax import lax
from jax.experimental import pallas as pl
from jax.experimental.pallas import tpu as pltpu

F32 = jnp.float32
BF16 = jnp.bfloat16

MLA_HEADS = 16
MLA_Q_RANK = 1024
MLA_KV_RANK = 512
MLA_NOPE = 128
MLA_ROPE = 64
MLA_V = 128
ROPE_THETA = 10000.0
FOX_HEADS = 16
FOX_DIM = 128
PEER_HEADS = 8
PEER_KEYS = 128
PEER_HALF = 128
PEER_TOPK = 16
LN_EPS = 1e-5
RMS_EPS = 1e-6
LANES = 128
BF16_SUBLANES = 16
LOG2E = math.log2(math.e)
NEG = -1e30
POS_SENTINEL = 1e9
ATTN_CHUNK = 1024
ATTN_SUBBLOCKS = 2

VMEM_LIMIT = 56 * 1024 * 1024


def _cparams(n_axes):
    return pltpu.CompilerParams(dimension_semantics=("arbitrary",) * n_axes,
                                vmem_limit_bytes=VMEM_LIMIT)


def _mm_kernel(x_ref, w_ref, o_ref):
    o_ref[...] = jnp.dot(x_ref[...], w_ref[...], preferred_element_type=F32).astype(o_ref.dtype)


def _mm_scale_kernel(x_ref, w_ref, s_ref, o_ref):
    acc = jnp.dot(x_ref[...], w_ref[...], preferred_element_type=F32)
    o_ref[...] = (acc * s_ref[...]).astype(o_ref.dtype)


def _wspec(layer, block, index_map):
    if layer is None:
        return pl.BlockSpec(block, index_map)
    return pl.BlockSpec((None,) + block, lambda *g: (layer,) + index_map(*g))


def _mm(x, w, out_dtype, tm, tn, colscale=None, name="mm", layer=None):
    m, k = x.shape
    n = w.shape[-1]
    tm, tn = min(tm, m), min(tn, n)
    assert m % tm == 0 and n % tn == 0
    in_specs = [pl.BlockSpec((tm, k), lambda i, j: (i, 0)),
                _wspec(layer, (k, tn), lambda i, j: (0, j))]
    args = [x, w]
    body = _mm_kernel
    if colscale is not None:
        in_specs.append(pl.BlockSpec((1, tn), lambda i, j: (0, j)))
        args.append(colscale)
        body = _mm_scale_kernel
    return pl.pallas_call(
        body,
        grid=(m // tm, n // tn),
        in_specs=in_specs,
        out_specs=pl.BlockSpec((tm, tn), lambda i, j: (i, j)),
        out_shape=jax.ShapeDtypeStruct((m, n), out_dtype),
        compiler_params=_cparams(2),
        name=name,
    )(*args)


def _rope_table_kernel(pos_ref, c_ref, cos_ref, sina_ref, sinb_ref):
    ang = pos_ref[...].astype(F32) * c_ref[0:1, :]
    cos = jnp.cos(ang)
    sin = jnp.sin(ang)
    cos_ref[...] = cos * c_ref[1:2, :]
    sina_ref[...] = sin * c_ref[2:3, :]
    sinb_ref[...] = sin * c_ref[3:4, :]


def _rope_tables(positions_col):
    n = positions_col.shape[0]
    tm = min(1024, n)
    half = MLA_ROPE // 2
    inv_freq = (ROPE_THETA ** (-np.arange(half, dtype=np.float32) / half)).astype(np.float32)
    consts = np.zeros((8, LANES), np.float32)
    consts[0, :half] = inv_freq
    consts[0, half:2 * half] = inv_freq
    consts[1, :2 * half] = 1.0
    consts[2, :half] = -1.0
    consts[3, half:2 * half] = 1.0
    shp = jax.ShapeDtypeStruct((n, LANES), F32)
    return pl.pallas_call(
        _rope_table_kernel,
        grid=(n // tm,),
        in_specs=[pl.BlockSpec((tm, 1), lambda i: (i, 0)),
                  pl.BlockSpec((8, LANES), lambda i: (0, 0))],
        out_specs=[pl.BlockSpec((tm, LANES), lambda i: (i, 0))] * 3,
        out_shape=[shp, shp, shp],
        compiler_params=_cparams(1),
        name="rope_tables",
    )(positions_col, jnp.asarray(consts))


def _rope128(x, cos, sina, sinb):
    return (x * cos + pltpu.roll(x, 3 * (MLA_ROPE // 2), 1) * sina
            + pltpu.roll(x, MLA_ROPE // 2, 1) * sinb)


def _rms(x, g):
    return x * lax.rsqrt(jnp.mean(x * x, axis=-1, keepdims=True) + RMS_EPS) * g


def _mla_q_kernel(cq_ref, g_ref, wn_ref, wr_ref, cos_ref, sina_ref, sinb_ref, q_ref, *, scale):
    xn = _rms(cq_ref[...], g_ref[...]).astype(BF16)
    qn = jnp.dot(xn, wn_ref[...], preferred_element_type=F32) * scale
    qr = jnp.dot(xn, wr_ref[...], preferred_element_type=F32) * scale
    cos, sina, sinb = cos_ref[...], sina_ref[...], sinb_ref[...]
    for hh in range(2):
        lo = hh * LANES
        q_ref[:, 2 * lo:2 * lo + LANES] = qn[:, lo:lo + LANES].astype(q_ref.dtype)
        q_ref[:, 2 * lo + LANES:2 * lo + 2 * LANES] = _rope128(
            qr[:, lo:lo + LANES], cos, sina, sinb).astype(q_ref.dtype)


def _mla_q(zs, g_cq, wqn, wqr, tabs):
    n = zs.shape[0]
    tm = min(512, n)
    pairs = MLA_HEADS // 2
    scale = (MLA_NOPE + MLA_ROPE) ** -0.5 * LOG2E
    tab_spec = pl.BlockSpec((tm, LANES), lambda i, p: (i, 0))
    return pl.pallas_call(
        functools.partial(_mla_q_kernel, scale=scale),
        grid=(n // tm, pairs),
        in_specs=[pl.BlockSpec((tm, MLA_Q_RANK), lambda i, p: (i, 0)),
                  pl.BlockSpec((1, MLA_Q_RANK), lambda i, p: (0, 0)),
                  pl.BlockSpec((MLA_Q_RANK, 2 * LANES), lambda i, p: (0, p)),
                  pl.BlockSpec((MLA_Q_RANK, 2 * LANES), lambda i, p: (0, p)),
                  tab_spec, tab_spec, tab_spec],
        out_specs=pl.BlockSpec((tm, 4 * LANES), lambda i, p: (i, p)),
        out_shape=jax.ShapeDtypeStruct((n, MLA_HEADS * 2 * LANES), BF16),
        compiler_params=_cparams(2),
        name="mla_q",
    )(zs, g_cq, wqn, wqr, *tabs)


def _mla_kv_kernel(ckv_ref, g_ref, kr_ref, wk_ref, wv_ref, cos_ref, sina_ref, sinb_ref,
                   k_ref, v_ref):
    xn = _rms(ckv_ref[...], g_ref[...]).astype(BF16)
    kn = jnp.dot(xn, wk_ref[...], preferred_element_type=F32)
    v_ref[...] = jnp.dot(xn, wv_ref[...], preferred_element_type=F32).astype(v_ref.dtype)
    kr = _rope128(kr_ref[...], cos_ref[...], sina_ref[...], sinb_ref[...]).astype(k_ref.dtype)
    for hh in range(2):
        lo = hh * LANES
        k_ref[:, 2 * lo:2 * lo + LANES] = kn[:, lo:lo + LANES].astype(k_ref.dtype)
        k_ref[:, 2 * lo + LANES:2 * lo + 2 * LANES] = kr


def _mla_kv(zs, g_ckv, wuk, wuv, tabs):
    n = zs.shape[0]
    tm = min(512, n)
    pairs = MLA_HEADS // 2
    ckv_blk = MLA_Q_RANK // MLA_KV_RANK
    kr_blk = (MLA_Q_RANK + MLA_KV_RANK) // LANES
    tab_spec = pl.BlockSpec((tm, LANES), lambda i, p: (i, 0))
    return pl.pallas_call(
        _mla_kv_kernel,
        grid=(n // tm, pairs),
        in_specs=[pl.BlockSpec((tm, MLA_KV_RANK), lambda i, p: (i, ckv_blk)),
                  pl.BlockSpec((1, MLA_KV_RANK), lambda i, p: (0, 0)),
                  pl.BlockSpec((tm, LANES), lambda i, p: (i, kr_blk)),
                  pl.BlockSpec((MLA_KV_RANK, 2 * LANES), lambda i, p: (0, p)),
                  pl.BlockSpec((MLA_KV_RANK, 2 * LANES), lambda i, p: (0, p)),
                  tab_spec, tab_spec, tab_spec],
        out_specs=[pl.BlockSpec((tm, 4 * LANES), lambda i, p: (i, p)),
                   pl.BlockSpec((tm, 2 * LANES), lambda i, p: (i, p))],
        out_shape=[jax.ShapeDtypeStruct((n, MLA_HEADS * 2 * LANES), BF16),
                   jax.ShapeDtypeStruct((n, MLA_HEADS * MLA_V), BF16)],
        compiler_params=_cparams(2),
        name="mla_kv",
    )(zs, g_ckv, zs, wuk, wuv, *tabs)


def _split3(x):
    hi = x.astype(BF16)
    r = x - hi.astype(F32)
    mid = r.astype(BF16)
    lo = (r - mid.astype(F32)).astype(BF16)
    return hi, mid, lo


def _cum_kernel(fl_ref, bf_ref, tri_ref, o_ref, carry_ref):
    @pl.when(pl.program_id(0) == 0)
    def _():
        carry_ref[...] = jnp.zeros_like(carry_ref)

    x = fl_ref[...] + bf_ref[...]
    lf = jnp.minimum(x, 0.0) - jnp.log1p(jnp.exp(-jnp.abs(x)))
    tri = tri_ref[...]
    c = carry_ref[...]
    for part in _split3(lf):
        c = c + jnp.dot(tri, part, preferred_element_type=F32)
    carry_ref[...] = c[c.shape[0] - 1:, :]
    o_ref[...] = c


def _fox_cum(zs, bf_row):
    n = zs.shape[0]
    tm = 256
    fl_blk = (MLA_Q_RANK + MLA_KV_RANK) // LANES + 1
    tri = jnp.asarray(np.tril(np.ones((tm, tm), np.float32)), dtype=BF16)
    return pl.pallas_call(
        _cum_kernel,
        grid=(n // tm,),
        in_specs=[pl.BlockSpec((tm, LANES), lambda i: (i, fl_blk)),
                  pl.BlockSpec((1, LANES), lambda i: (0, 0)),
                  pl.BlockSpec((tm, tm), lambda i: (0, 0))],
        out_specs=pl.BlockSpec((tm, LANES), lambda i: (i, 0)),
        out_shape=jax.ShapeDtypeStruct((n, LANES), F32),
        scratch_shapes=[pltpu.VMEM((1, LANES), F32)],
        compiler_params=_cparams(1),
        name="fox_cum",
    )(zs, bf_row, tri)


def _fox_pack_kernel(fq_ref, fk_ref, cum_ref, e_ref, q_ref, k_ref):
    c3 = jnp.concatenate(_split3(cum_ref[...] * LOG2E), axis=1)
    kx = jnp.dot(c3, e_ref[0], preferred_element_type=F32)
    lane = lax.broadcasted_iota(jnp.int32, (fq_ref.shape[0], LANES), 1)
    qaug = jnp.where(lane < 3, -1.0, 0.0).astype(q_ref.dtype)
    for hh in range(2):
        lo = hh * LANES
        q_ref[:, 2 * lo:2 * lo + LANES] = fq_ref[:, lo:lo + LANES]
        q_ref[:, 2 * lo + LANES:2 * lo + 2 * LANES] = qaug
        k_ref[:, 2 * lo:2 * lo + LANES] = fk_ref[:, lo:lo + LANES]
        k_ref[:, 2 * lo + LANES:2 * lo + 2 * LANES] = kx[:, 2 * lo + LANES:2 * lo + 2 * LANES].astype(k_ref.dtype)


def _fox_pack(zb, cum):
    n = zb.shape[0]
    tm = min(1024, n)
    pairs = FOX_HEADS // 2
    place = np.zeros((pairs, 3 * LANES, 4 * LANES), np.float32)
    for p in range(pairs):
        for hh in range(2):
            for part in range(3):
                place[p, part * LANES + 2 * p + hh, hh * 2 * LANES + LANES + part] = 1.0
    out = jax.ShapeDtypeStruct((n, FOX_HEADS * 2 * LANES), BF16)
    ospec = pl.BlockSpec((tm, 4 * LANES), lambda i, p: (i, p))
    return pl.pallas_call(
        _fox_pack_kernel,
        grid=(n // tm, pairs),
        in_specs=[pl.BlockSpec((tm, 2 * LANES), lambda i, p: (i, p)),
                  pl.BlockSpec((tm, 2 * LANES), lambda i, p: (i, pairs + p)),
                  pl.BlockSpec((tm, LANES), lambda i, p: (i, 0)),
                  pl.BlockSpec((1, 3 * LANES, 4 * LANES), lambda i, p: (p, 0, 0))],
        out_specs=[ospec, ospec],
        out_shape=[out, out],
        compiler_params=_cparams(2),
        name="fox_pack",
    )(zb, zb, cum, jnp.asarray(place, dtype=BF16))


def _attn_kernel(q_ref, k_ref, v_ref, o_ref, vt_sc, s_sc, m_sc, acc_sc, *, tc, nsub, dv):
    i = pl.program_id(1)
    nchunks, dva, _ = vt_sc.shape

    @pl.when(i == 0)
    def _():
        ones_row = (lax.broadcasted_iota(jnp.int32, (dva - dv, tc), 0) == 0).astype(vt_sc.dtype)

        def fill(c, carry):
            off = pl.multiple_of(c * tc, tc)
            vt_sc[c, 0:dv, :] = v_ref[pl.ds(off, tc), :].astype(F32).T.astype(vt_sc.dtype)
            vt_sc[c, dv:dva, :] = ones_row
            return carry

        lax.fori_loop(0, nchunks, fill, 0)

    m_sc[...] = jnp.full_like(m_sc, NEG)
    acc_sc[...] = jnp.zeros_like(acc_sc)

    def scores(j, rs, slot):
        off = pl.multiple_of(j * tc, tc)
        k = k_ref[pl.ds(off, tc), :]
        for r in rs:
            s_sc[slot, r] = lax.dot_general(k, q_ref[r * tc:(r + 1) * tc, :],
                                            (((1,), (1,)), ((), ())), preferred_element_type=F32)

    def update(j, rs, slot, masked_r):
        for r in rs:
            s = s_sc[slot, r]
            if r == masked_r:
                key = lax.broadcasted_iota(jnp.int32, (tc, tc), 0)
                qry = lax.broadcasted_iota(jnp.int32, (tc, tc), 1)
                s = jnp.where(key <= qry, s, NEG)
            m_prev = m_sc[r]
            m_new = jnp.maximum(m_prev, jnp.max(s, axis=0, keepdims=True))
            alpha = jnp.exp2(m_prev - m_new)
            p = jnp.exp2(s - m_new).astype(vt_sc.dtype)
            acc_sc[r] = alpha * acc_sc[r] + jnp.dot(vt_sc[j], p, preferred_element_type=F32)
            m_sc[r] = m_new

    all_r = range(nsub)
    n_main = i * nsub
    scores(0, all_r, 0)

    def body(t, carry):
        j = 2 * t
        scores(j + 1, all_r, 1)
        update(j, all_r, 0, -1)
        scores(j + 2, all_r, 0)
        update(j + 1, all_r, 1, -1)
        return carry

    lax.fori_loop(0, n_main // 2, body, 0)
    for jj in range(nsub):
        if jj + 1 < nsub:
            scores(n_main + jj + 1, range(jj + 1, nsub), (jj + 1) % 2)
        update(n_main + jj, range(jj, nsub), jj % 2, jj)
    for r in range(nsub):
        acc = acc_sc[r]
        o = acc[0:dv, :] / acc[dv:dv + 1, :]
        o_ref[r * tc:(r + 1) * tc, :] = o.T.astype(o_ref.dtype)


def _attention(q_arr, k_arr, v_arr, heads, dk, dv, q_blk0, k_blk0, v_blk0, name):
    n = q_arr.shape[0]
    tc = min(ATTN_CHUNK, n)
    nsub = min(ATTN_SUBBLOCKS, n // tc)
    tq = tc * nsub
    assert nsub % 2 == 0 and n % tq == 0
    dva = dv + BF16_SUBLANES
    return pl.pallas_call(
        functools.partial(_attn_kernel, tc=tc, nsub=nsub, dv=dv),
        grid=(heads, n // tq),
        in_specs=[pl.BlockSpec((tq, dk), lambda h, i: (i, q_blk0 + h)),
                  pl.BlockSpec((n, dk), lambda h, i: (0, k_blk0 + h)),
                  pl.BlockSpec((n, dv), lambda h, i: (0, v_blk0 + h))],
        out_specs=pl.BlockSpec((tq, dv), lambda h, i: (i, h)),
        out_shape=jax.ShapeDtypeStruct((n, heads * dv), BF16),
        scratch_shapes=[pltpu.VMEM((n // tc, dva, tc), BF16),
                        pltpu.VMEM((2, nsub, tc, tc), F32),
                        pltpu.VMEM((nsub, 1, tc), F32),
                        pltpu.VMEM((nsub, dva, tc), F32)],
        compiler_params=_cparams(2),
        name=name,
    )(q_arr, k_arr, v_arr)


def _merge_kernel(ya_ref, yb_ref, wa_ref, wb_ref, ga_ref, gb_ref, o_ref):
    a = jnp.dot(ya_ref[...], wa_ref[...], preferred_element_type=F32)
    b = jnp.dot(yb_ref[...], wb_ref[...], preferred_element_type=F32)
    ga = jax.nn.sigmoid(ga_ref[...].astype(F32))
    gb = jax.nn.sigmoid(gb_ref[...].astype(F32))
    o_ref[...] = (ga * a + gb * b).astype(o_ref.dtype)


def _merge(ya, yb, w_ya, w_yb, zb, ga_col0, gb_col0, layer):
    n, ka = ya.shape
    kb = yb.shape[1]
    d = w_ya.shape[-1]
    tm, tn = min(1024, n), 512
    ga0, gb0 = ga_col0 // tn, gb_col0 // tn
    return pl.pallas_call(
        _merge_kernel,
        grid=(n // tm, d // tn),
        in_specs=[pl.BlockSpec((tm, ka), lambda i, j: (i, 0)),
                  pl.BlockSpec((tm, kb), lambda i, j: (i, 0)),
                  _wspec(layer, (ka, tn), lambda i, j: (0, j)),
                  _wspec(layer, (kb, tn), lambda i, j: (0, j)),
                  pl.BlockSpec((tm, tn), lambda i, j: (i, ga0 + j)),
                  pl.BlockSpec((tm, tn), lambda i, j: (i, gb0 + j))],
        out_specs=pl.BlockSpec((tm, tn), lambda i, j: (i, j)),
        out_shape=jax.ShapeDtypeStruct((n, d), BF16),
        compiler_params=_cparams(2),
        name="merge",
    )(ya, yb, w_ya, w_yb, zb, zb)


def _ln_kernel(*refs, alpha, n_add, with_t):
    h_ref = refs[0]
    add_refs = refs[1:1 + n_add]
    g_ref, b_ref, o_ref, ob_ref = refs[1 + n_add:5 + n_add]
    x = alpha * h_ref[...]
    for r in add_refs:
        x = x + r[...].astype(F32)
    mu = jnp.mean(x, axis=-1, keepdims=True)
    xc = x - mu
    var = jnp.mean(xc * xc, axis=-1, keepdims=True)
    y = xc * lax.rsqrt(var + LN_EPS) * g_ref[...] + b_ref[...]
    o_ref[...] = y
    ob_ref[...] = y.astype(ob_ref.dtype)
    if with_t:
        obt_ref = refs[5 + n_add]
        obt_ref[...] = y.T.astype(obt_ref.dtype)


def _ln(h, addends, g, b, alpha, with_t):
    n, d = h.shape
    tm = min(256, n)
    row = pl.BlockSpec((tm, d), lambda i: (i, 0))
    vec = pl.BlockSpec((1, d), lambda i: (0, 0))
    out_specs = [row, row]
    out_shape = [jax.ShapeDtypeStruct((n, d), F32), jax.ShapeDtypeStruct((n, d), BF16)]
    if with_t:
        out_specs.append(pl.BlockSpec((d, tm), lambda i: (0, i)))
        out_shape.append(jax.ShapeDtypeStruct((d, n), BF16))
    return pl.pallas_call(
        functools.partial(_ln_kernel, alpha=alpha, n_add=len(addends), with_t=with_t),
        grid=(n // tm,),
        in_specs=[row] * (1 + len(addends)) + [vec, vec],
        out_specs=out_specs,
        out_shape=out_shape,
        compiler_params=_cparams(1),
        name="deepnorm_ln",
    )(h, *addends, g, b)


def _topk_rows(problems, k, pos=None):
    r, t = problems[0][0].shape
    iota = lax.broadcasted_iota(jnp.int32, (r, t), 0).astype(F32) if pos is None else pos
    for _, _, _, rank_ref in problems:
        if rank_ref is not None:
            rank_ref[...] = jnp.full((r, t), float(k), F32)

    def body(kk, carry):
        for x_ref, val_ref, idx_ref, rank_ref in problems:
            x = x_ref[...]
            m = jnp.max(x, axis=0, keepdims=True)
            idx = jnp.min(jnp.where(x == m, iota, POS_SENTINEL), axis=0, keepdims=True)
            sel = iota == idx
            x_ref[...] = jnp.where(sel, -jnp.inf, x)
            if rank_ref is not None:
                rank_ref[...] = jnp.where(sel, lax.convert_element_type(kk, F32), rank_ref[...])
            val_ref[pl.ds(kk, 1), :] = m
            idx_ref[pl.ds(kk, 1), :] = idx
        return carry

    lax.fori_loop(0, k, body, 0)


def _candidate_groups(k):
    assert k == 16
    return [(0, 16), (1, 8), (2, 8), (3, 8), (4, 4), (5, 4), (6, 4), (7, 4)] + [(a, 1) for a in range(8, 16)]


def _route_kernel(qp_ref, k1_ref, k2_ref, cpos_ref, nofi_ref, c_ref, r2_ref, e2_ref,
                  s_sc, cand_sc, t1_sc, i1_sc, t2_sc, i2_sc, top_sc, pos_sc, r1_sc, r2_sc):
    kk = PEER_TOPK
    qp = qp_ref[...].astype(BF16)
    nt = (((1,), (1,)), ((), ()))
    s1 = lax.dot_general(k1_ref[0].astype(BF16), qp[:, :PEER_HALF], nt, preferred_element_type=F32)
    s2 = lax.dot_general(k2_ref[0].astype(BF16), qp[:, PEER_HALF:], nt, preferred_element_type=F32)

    s_sc[0] = s1
    s_sc[1] = s2
    _topk_rows([(s_sc.at[0], t1_sc, i1_sc, r1_sc), (s_sc.at[1], t2_sc, i2_sc, r2_sc)], kk)

    row = 0
    for a, nb in _candidate_groups(kk):
        cand_sc[row:row + nb, :] = t1_sc[a:a + 1, :] + t2_sc[0:nb, :]
        row += nb
    _topk_rows([(cand_sc, top_sc, pos_sc, None)], kk, pos=cpos_ref[...])

    top = top_sc[...]
    z = jnp.sum(jnp.exp(top - top[0:1, :]), axis=0, keepdims=True)
    a_sel = jnp.floor(pos_sc[...] * (1.0 / kk))
    r1 = r1_sc[...]
    nofi = jnp.zeros_like(r1)
    for a in range(kk):
        n_a = jnp.sum((a_sel == float(a)).astype(F32), axis=0, keepdims=True)
        nofi = nofi + jnp.where(r1 == float(a), n_a, 0.0)
    nofi_ref[0] = nofi
    c_ref[0] = jnp.exp(s1 - t1_sc[0:1, :]) / z
    r2_ref[0] = r2_sc[...].astype(r2_ref.dtype)
    e2_ref[0] = jnp.exp(s2 - t2_sc[0:1, :]).astype(e2_ref.dtype)


def _route(qp, sub_k1, sub_k2):
    n = qp.shape[0]
    tr = min(512, n)
    kk = PEER_TOPK
    out = jax.ShapeDtypeStruct((PEER_HEADS, PEER_KEYS, n), F32)
    out_b = jax.ShapeDtypeStruct((PEER_HEADS, PEER_KEYS, n), BF16)
    ospec = pl.BlockSpec((1, PEER_KEYS, tr), lambda i, h: (h, 0, i))
    kspec = pl.BlockSpec((1, PEER_KEYS, PEER_HALF), lambda i, h: (h, 0, 0))
    small = pltpu.VMEM((kk, tr), F32)
    cpos = np.concatenate([a * kk + np.arange(nb) for a, nb in _candidate_groups(kk)]).astype(np.float32)
    ncand = cpos.shape[0]
    cpos = jnp.asarray(np.broadcast_to(cpos[:, None], (ncand, tr)))
    return pl.pallas_call(
        _route_kernel,
        grid=(n // tr, PEER_HEADS),
        in_specs=[pl.BlockSpec((tr, 2 * PEER_HALF), lambda i, h: (i, h)), kspec, kspec,
                  pl.BlockSpec((ncand, tr), lambda i, h: (0, 0))],
        out_specs=[ospec] * 4,
        out_shape=[out, out, out_b, out_b],
        scratch_shapes=[pltpu.VMEM((2, PEER_KEYS, tr), F32), pltpu.VMEM((ncand, tr), F32),
                        small, small, small, small, small, small,
                        pltpu.VMEM((PEER_KEYS, tr), F32), pltpu.VMEM((PEER_KEYS, tr), F32)],
        compiler_params=_cparams(2),
        name="peer_route",
    )(qp, sub_k1, sub_k2, cpos)


def _gelu_tanh(x):
    c = math.sqrt(2.0 / math.pi)
    return 0.5 * x * (1.0 + jnp.tanh(c * (x + 0.044715 * (x * x * x))))


def _gates(nofi_ref, c_ref, r2_ref, e2_ref, w_sc):
    for il in range(nofi_ref.shape[0]):
        w = None
        for h in range(PEER_HEADS):
            n_i = nofi_ref[il, h:h + 1, :].astype(BF16)
            c_i = c_ref[il, h:h + 1, :].astype(BF16)
            term = jnp.where(r2_ref[h] < n_i, e2_ref[h] * c_i, jnp.zeros((), BF16))
            w = term if w is None else w + term
        w_sc[il * PEER_KEYS:(il + 1) * PEER_KEYS, :] = w


def _expert_kernel(u_ref, vt_ref, ht_ref, nofi0_ref, c0_ref, nofi_ref, c_ref, r2_ref, e2_ref, o_ref,
                   acc_sc, g_sc, w_sc):
    e = pl.program_id(1)
    old = lax.rem(e + 1, 2)
    new = lax.rem(e, 2)

    @pl.when(e == 0)
    def _():
        acc_sc[...] = jnp.zeros_like(acc_sc)
        g_sc[...] = jnp.zeros_like(g_sc)
        _gates(nofi0_ref, c0_ref, r2_ref, e2_ref, w_sc.at[0])

    act = jnp.dot(u_ref[...], ht_ref[...], preferred_element_type=F32)
    acc_sc[...] += jnp.dot(vt_ref[...], g_sc[old], preferred_element_type=F32)
    g_sc[new] = _gelu_tanh(act).astype(g_sc.dtype) * w_sc[new]
    _gates(nofi_ref, c_ref, r2_ref, e2_ref, w_sc.at[old])

    @pl.when(e == pl.num_programs(1) - 1)
    def _():
        o_ref[...] = acc_sc[...].T.astype(o_ref.dtype)


def _experts(u_b, vt_b, ht_b, nofi_t, c_t, r2, e2, layer=None):
    d, n = ht_b.shape
    ne = u_b.shape[-2]
    tm = min(512, n)
    te = 512
    gi = te // PEER_KEYS
    nblk = ne // te
    cur = lambda e: jnp.minimum(e, nblk - 1)
    nxt = lambda e: jnp.minimum(e + 1, nblk - 1)
    prev = lambda e: jnp.maximum(e - 1, 0)
    ispec0 = pl.BlockSpec((gi, PEER_HEADS, tm), lambda b, e: (0, 0, b))
    ispec = pl.BlockSpec((gi, PEER_HEADS, tm), lambda b, e: (nxt(e), 0, b))
    jspec = pl.BlockSpec((PEER_HEADS, PEER_KEYS, tm), lambda b, e: (0, 0, b))
    return pl.pallas_call(
        _expert_kernel,
        grid=(n // tm, nblk + 1),
        in_specs=[_wspec(layer, (te, d), lambda b, e: (cur(e), 0)),
                  _wspec(layer, (d, te), lambda b, e: (0, prev(e))),
                  pl.BlockSpec((d, tm), lambda b, e: (0, b)),
                  ispec0, ispec0, ispec, ispec, jspec, jspec],
        out_specs=pl.BlockSpec((tm, d), lambda b, e: (b, 0)),
        out_shape=jax.ShapeDtypeStruct((n, d), BF16),
        scratch_shapes=[pltpu.VMEM((d, tm), F32), pltpu.VMEM((2, te, tm), BF16),
                        pltpu.VMEM((2, te, tm), BF16)],
        compiler_params=_cparams(2),
        name="peer_experts",
    )(u_b, vt_b, ht_b, nofi_t, c_t, nofi_t, c_t, r2, e2)


def _ple_kernel(h_ref, wg_ref, p_ref, wp_ref, o_ref):
    g = jnp.dot(h_ref[...], wg_ref[...], preferred_element_type=F32)
    pp = jnp.dot(p_ref[...], wp_ref[...], preferred_element_type=F32)
    o_ref[...] = (jax.nn.sigmoid(g) * pp).astype(o_ref.dtype)


def _ple(h_b, w_pg, p_b, w_pe, layer):
    n, d = h_b.shape
    kp = p_b.shape[-1]
    tm, tn = min(1024, n), 1024
    return pl.pallas_call(
        _ple_kernel,
        grid=(n // tm, d // tn),
        in_specs=[pl.BlockSpec((tm, d), lambda i, j: (i, 0)),
                  _wspec(layer, (d, tn), lambda i, j: (0, j)),
                  _wspec(layer, (tm, kp), lambda i, j: (i, 0)),
                  _wspec(layer, (kp, tn), lambda i, j: (0, j))],
        out_specs=pl.BlockSpec((tm, tn), lambda i, j: (i, j)),
        out_shape=jax.ShapeDtypeStruct((n, d), BF16),
        compiler_params=_cparams(2),
        name="ple",
    )(h_b, w_pg, p_b, w_pe)


def _mixer_weights(w_in, w_uq, w_ukv, b_f):
    depth, d, _ = w_in.shape
    sizes = (MLA_Q_RANK, MLA_KV_RANK, MLA_ROPE, FOX_HEADS * FOX_DIM, FOX_HEADS * FOX_DIM,
             FOX_HEADS * FOX_DIM, FOX_HEADS, d, d)
    offs = np.concatenate([[0], np.cumsum(sizes)])
    cq, ckv, kr, _, _, _, fl, _, _ = [w_in[:, :, offs[t]:offs[t + 1]] for t in range(9)]
    zpad = lambda c: jnp.zeros((depth, d, c), w_in.dtype)
    w_small = jnp.concatenate([cq, ckv, kr, zpad(LANES - MLA_ROPE), fl, zpad(LANES - FOX_HEADS)],
                              axis=2).astype(BF16)
    w_fox = w_in[:, :, offs[3]:offs[6]].astype(BF16)
    w_gate = w_in[:, :, offs[7]:offs[9]].astype(BF16)
    colscale = np.ones((1, w_fox.shape[2]), np.float32)
    colscale[0, :FOX_HEADS * FOX_DIM] = FOX_DIM ** -0.5 * LOG2E
    qk = MLA_NOPE + MLA_ROPE
    wq3 = w_uq.reshape(depth, MLA_Q_RANK, MLA_HEADS, qk)
    wqn = wq3[..., :MLA_NOPE].reshape(depth, MLA_Q_RANK, MLA_HEADS * MLA_NOPE).astype(BF16)
    wqr = jnp.pad(wq3[..., MLA_NOPE:], ((0, 0), (0, 0), (0, 0), (0, LANES - MLA_ROPE))
                  ).reshape(depth, MLA_Q_RANK, MLA_HEADS * LANES).astype(BF16)
    wkv3 = w_ukv.reshape(depth, MLA_KV_RANK, MLA_HEADS, MLA_NOPE + MLA_V)
    wuk = wkv3[..., :MLA_NOPE].reshape(depth, MLA_KV_RANK, MLA_HEADS * MLA_NOPE).astype(BF16)
    wuv = wkv3[..., MLA_NOPE:].reshape(depth, MLA_KV_RANK, MLA_HEADS * MLA_V).astype(BF16)
    bf_rows = jnp.pad(b_f, ((0, 0), (0, LANES - FOX_HEADS))).reshape(depth, 1, LANES)
    return w_small, w_fox, w_gate, jnp.asarray(colscale), wqn, wqr, wuk, wuv, bf_rows


def kernel(x, p, positions, w_in, g_cq, g_ckv, w_uq, w_ukv, b_f, w_ya, w_yb, w_o, ln1_g, ln1_b,
           w_pq, sub_k1, sub_k2, u_tab, v_tab, w_pg, w_pe, ln2_g, ln2_b):
    bsz, seq, d = x.shape
    depth = w_in.shape[0]
    n = bsz * seq
    assert bsz == 1, "token-major layout assumes a single sequence"
    alpha = (2 * depth) ** 0.25
    fox_w = FOX_HEADS * FOX_DIM

    h = x.reshape(n, d)
    h_b = h.astype(BF16)
    tabs = _rope_tables(positions.reshape(n, 1))

    w_small, w_fox, w_gate, colscale, wqn, wqr, wuk, wuv, bf_rows = _mixer_weights(w_in, w_uq, w_ukv, b_f)
    w_ya_b, w_yb_b, w_o_b = w_ya.astype(BF16), w_yb.astype(BF16), w_o.astype(BF16)
    w_pq_b, w_pg_b, w_pe_b = w_pq.astype(BF16), w_pg.astype(BF16), w_pe.astype(BF16)
    u_b = u_tab.astype(BF16)
    vt_b = v_tab.transpose(0, 2, 1).astype(BF16)
    p_b = p.reshape(depth, n, -1).astype(BF16)

    for i in range(depth):
        zs = _mm(h_b, w_small, F32, 1024, 256, name="in_proj_small", layer=i)
        zf = _mm(h_b, w_fox, BF16, 1024, 1024, colscale, name="in_proj_fox", layer=i)
        zg = _mm(h_b, w_gate, BF16, 1024, 1024, name="in_proj_gate", layer=i)
        q_mla = _mla_q(zs, g_cq[i].reshape(1, -1), wqn[i], wqr[i], tabs)
        k_mla, v_mla = _mla_kv(zs, g_ckv[i].reshape(1, -1), wuk[i], wuv[i], tabs)
        y_a = _attention(q_mla, k_mla, v_mla, MLA_HEADS, 2 * LANES, MLA_V, 0, 0, 0, "attn_mla")
        q_fox, k_fox = _fox_pack(zf, _fox_cum(zs, bf_rows[i]))
        y_b = _attention(q_fox, k_fox, zf, FOX_HEADS, 2 * LANES, FOX_DIM, 0, 0, 2 * FOX_HEADS,
                         "attn_fox")
        merged = _merge(y_a, y_b, w_ya_b, w_yb_b, zg, 0, d, i)
        mix = _mm(merged, w_o_b, F32, 1024, 1024, name="out_proj", layer=i)
        h, h_b, ht_b = _ln(h, [mix], ln1_g[i].reshape(1, d), ln1_b[i].reshape(1, d), alpha, True)

        qp = _mm(h_b, w_pq_b, F32, 1024, 1024, name="peer_query", layer=i)
        nofi, c, r2, e2 = _route(qp, sub_k1[i], sub_k2[i])
        ffn = _experts(u_b, vt_b, ht_b, nofi.transpose(1, 0, 2), c.transpose(1, 0, 2), r2, e2, i)
        ple = _ple(h_b, w_pg_b, p_b, w_pe_b, i)
        h, h_b = _ln(h, [ffn, ple], ln2_g[i].reshape(1, d), ln2_b[i].reshape(1, d), alpha, False)

    return h.reshape(bsz, seq, d)
```

```python
import functools
import math

import numpy as np
import jax
import jax.numpy as jnp
from jax import lax
from jax.experimental import pallas as pl
from jax.experimental.pallas import tpu as pltpu

F32 = jnp.float32
BF16 = jnp.bfloat16

MLA_HEADS = 16
MLA_Q_RANK = 1024
MLA_KV_RANK = 512
MLA_NOPE = 128
MLA_ROPE = 64
MLA_V = 128
ROPE_THETA = 10000.0
FOX_HEADS = 16
FOX_DIM = 128
PEER_HEADS = 8
PEER_KEYS = 128
PEER_HALF = 128
PEER_TOPK = 16
LN_EPS = 1e-5
RMS_EPS = 1e-6
LANES = 128
BF16_SUBLANES = 16
LOG2E = math.log2(math.e)
NEG = -1e30
POS_SENTINEL = 1e9
MLA_HEAD_GROUP = 4
ATTN_CHUNK = 1024
ATTN_SUBBLOCKS = 2

VMEM_LIMIT = 56 * 1024 * 1024


def _cparams(n_axes):
    return pltpu.CompilerParams(dimension_semantics=("arbitrary",) * n_axes,
                                vmem_limit_bytes=VMEM_LIMIT)


def _mm_kernel(x_ref, w_ref, o_ref):
    o_ref[...] = jnp.dot(x_ref[...], w_ref[...], preferred_element_type=F32).astype(o_ref.dtype)


def _mm_scale_kernel(x_ref, w_ref, s_ref, o_ref):
    acc = jnp.dot(x_ref[...], w_ref[...], preferred_element_type=F32)
    o_ref[...] = (acc * s_ref[...]).astype(o_ref.dtype)


def _wspec(layer, block, index_map):
    if layer is None:
        return pl.BlockSpec(block, index_map)
    return pl.BlockSpec((None,) + block, lambda *g: (layer,) + index_map(*g))


def _mm(x, w, out_dtype, tm, tn, colscale=None, name="mm", layer=None):
    m, k = x.shape
    n = w.shape[-1]
    tm, tn = min(tm, m), min(tn, n)
    assert m % tm == 0 and n % tn == 0
    in_specs = [pl.BlockSpec((tm, k), lambda i, j: (i, 0)),
                _wspec(layer, (k, tn), lambda i, j: (0, j))]
    args = [x, w]
    body = _mm_kernel
    if colscale is not None:
        in_specs.append(pl.BlockSpec((1, tn), lambda i, j: (0, j)))
        args.append(colscale)
        body = _mm_scale_kernel
    return pl.pallas_call(
        body,
        grid=(m // tm, n // tn),
        in_specs=in_specs,
        out_specs=pl.BlockSpec((tm, tn), lambda i, j: (i, j)),
        out_shape=jax.ShapeDtypeStruct((m, n), out_dtype),
        compiler_params=_cparams(2),
        name=name,
    )(*args)


def _rope_table_kernel(pos_ref, c_ref, cos_ref, sina_ref, sinb_ref):
    ang = pos_ref[...].astype(F32) * c_ref[0:1, :]
    cos = jnp.cos(ang)
    sin = jnp.sin(ang)
    cos_ref[...] = cos * c_ref[1:2, :]
    sina_ref[...] = sin * c_ref[2:3, :]
    sinb_ref[...] = sin * c_ref[3:4, :]


def _rope_tables(positions_col):
    n = positions_col.shape[0]
    tm = min(1024, n)
    half = MLA_ROPE // 2
    inv_freq = (ROPE_THETA ** (-np.arange(half, dtype=np.float32) / half)).astype(np.float32)
    consts = np.zeros((8, LANES), np.float32)
    consts[0, :half] = inv_freq
    consts[0, half:2 * half] = inv_freq
    consts[1, :2 * half] = 1.0
    consts[2, :half] = -1.0
    consts[3, half:2 * half] = 1.0
    shp = jax.ShapeDtypeStruct((n, LANES), F32)
    return pl.pallas_call(
        _rope_table_kernel,
        grid=(n // tm,),
        in_specs=[pl.BlockSpec((tm, 1), lambda i: (i, 0)),
                  pl.BlockSpec((8, LANES), lambda i: (0, 0))],
        out_specs=[pl.BlockSpec((tm, LANES), lambda i: (i, 0))] * 3,
        out_shape=[shp, shp, shp],
        compiler_params=_cparams(1),
        name="rope_tables",
    )(positions_col, jnp.asarray(consts))


def _rope128(x, cos, sina, sinb):
    return (x * cos + pltpu.roll(x, 3 * (MLA_ROPE // 2), 1) * sina
            + pltpu.roll(x, MLA_ROPE // 2, 1) * sinb)


def _rms(x, g):
    return x * lax.rsqrt(jnp.mean(x * x, axis=-1, keepdims=True) + RMS_EPS) * g


def _mla_q_kernel(cq_ref, g_ref, wn_ref, wr_ref, cos_ref, sina_ref, sinb_ref, q_ref, *, scale):
    xn = _rms(cq_ref[...], g_ref[...]).astype(BF16)
    qn = jnp.dot(xn, wn_ref[...], preferred_element_type=F32) * scale
    qr = jnp.dot(xn, wr_ref[...], preferred_element_type=F32) * scale
    cos, sina, sinb = cos_ref[...], sina_ref[...], sinb_ref[...]
    for hh in range(wn_ref.shape[1] // LANES):
        lo = hh * LANES
        q_ref[:, 2 * lo:2 * lo + LANES] = qn[:, lo:lo + LANES].astype(q_ref.dtype)
        q_ref[:, 2 * lo + LANES:2 * lo + 2 * LANES] = _rope128(
            qr[:, lo:lo + LANES], cos, sina, sinb).astype(q_ref.dtype)


def _mla_q(zs, g_cq, wqn, wqr, tabs):
    n = zs.shape[0]
    tm = min(1024, n)
    grp = MLA_HEAD_GROUP
    scale = (MLA_NOPE + MLA_ROPE) ** -0.5 * LOG2E
    tab_spec = pl.BlockSpec((tm, LANES), lambda i, p: (i, 0))
    return pl.pallas_call(
        functools.partial(_mla_q_kernel, scale=scale),
        grid=(n // tm, MLA_HEADS // grp),
        in_specs=[pl.BlockSpec((tm, MLA_Q_RANK), lambda i, p: (i, 0)),
                  pl.BlockSpec((1, MLA_Q_RANK), lambda i, p: (0, 0)),
                  pl.BlockSpec((MLA_Q_RANK, grp * LANES), lambda i, p: (0, p)),
                  pl.BlockSpec((MLA_Q_RANK, grp * LANES), lambda i, p: (0, p)),
                  tab_spec, tab_spec, tab_spec],
        out_specs=pl.BlockSpec((tm, grp * 2 * LANES), lambda i, p: (i, p)),
        out_shape=jax.ShapeDtypeStruct((n, MLA_HEADS * 2 * LANES), BF16),
        compiler_params=_cparams(2),
        name="mla_q",
    )(zs, g_cq, wqn, wqr, *tabs)


def _mla_kv_kernel(ckv_ref, g_ref, kr_ref, wk_ref, wv_ref, cos_ref, sina_ref, sinb_ref,
                   k_ref, v_ref):
    xn = _rms(ckv_ref[...], g_ref[...]).astype(BF16)
    kn = jnp.dot(xn, wk_ref[...], preferred_element_type=F32)
    v_ref[...] = jnp.dot(xn, wv_ref[...], preferred_element_type=F32).astype(v_ref.dtype)
    kr = _rope128(kr_ref[...], cos_ref[...], sina_ref[...], sinb_ref[...]).astype(k_ref.dtype)
    for hh in range(wk_ref.shape[1] // LANES):
        lo = hh * LANES
        k_ref[:, 2 * lo:2 * lo + LANES] = kn[:, lo:lo + LANES].astype(k_ref.dtype)
        k_ref[:, 2 * lo + LANES:2 * lo + 2 * LANES] = kr


def _mla_kv(zs, g_ckv, wuk, wuv, tabs):
    n = zs.shape[0]
    tm = min(1024, n)
    grp = MLA_HEAD_GROUP
    ckv_blk = MLA_Q_RANK // MLA_KV_RANK
    kr_blk = (MLA_Q_RANK + MLA_KV_RANK) // LANES
    tab_spec = pl.BlockSpec((tm, LANES), lambda i, p: (i, 0))
    return pl.pallas_call(
        _mla_kv_kernel,
        grid=(n // tm, MLA_HEADS // grp),
        in_specs=[pl.BlockSpec((tm, MLA_KV_RANK), lambda i, p: (i, ckv_blk)),
                  pl.BlockSpec((1, MLA_KV_RANK), lambda i, p: (0, 0)),
                  pl.BlockSpec((tm, LANES), lambda i, p: (i, kr_blk)),
                  pl.BlockSpec((MLA_KV_RANK, grp * LANES), lambda i, p: (0, p)),
                  pl.BlockSpec((MLA_KV_RANK, grp * LANES), lambda i, p: (0, p)),
                  tab_spec, tab_spec, tab_spec],
        out_specs=[pl.BlockSpec((tm, grp * 2 * LANES), lambda i, p: (i, p)),
                   pl.BlockSpec((tm, grp * LANES), lambda i, p: (i, p))],
        out_shape=[jax.ShapeDtypeStruct((n, MLA_HEADS * 2 * LANES), BF16),
                   jax.ShapeDtypeStruct((n, MLA_HEADS * MLA_V), BF16)],
        compiler_params=_cparams(2),
        name="mla_kv",
    )(zs, g_ckv, zs, wuk, wuv, *tabs)


def _split3(x):
    hi = x.astype(BF16)
    r = x - hi.astype(F32)
    mid = r.astype(BF16)
    lo = (r - mid.astype(F32)).astype(BF16)
    return hi, mid, lo


def _cum_kernel(fl_ref, bf_ref, tri_ref, o_ref, carry_ref):
    @pl.when(pl.program_id(0) == 0)
    def _():
        carry_ref[...] = jnp.zeros_like(carry_ref)

    x = fl_ref[...] + bf_ref[...]
    lf = jnp.minimum(x, 0.0) - jnp.log1p(jnp.exp(-jnp.abs(x)))
    tri = tri_ref[...]
    c = carry_ref[...]
    for part in _split3(lf):
        c = c + jnp.dot(tri, part, preferred_element_type=F32)
    carry_ref[...] = c[c.shape[0] - 1:, :]
    o_ref[...] = c


def _fox_cum(zs, bf_row):
    n = zs.shape[0]
    tm = 256
    fl_blk = (MLA_Q_RANK + MLA_KV_RANK) // LANES + 1
    tri = jnp.asarray(np.tril(np.ones((tm, tm), np.float32)), dtype=BF16)
    return pl.pallas_call(
        _cum_kernel,
        grid=(n // tm,),
        in_specs=[pl.BlockSpec((tm, LANES), lambda i: (i, fl_blk)),
                  pl.BlockSpec((1, LANES), lambda i: (0, 0)),
                  pl.BlockSpec((tm, tm), lambda i: (0, 0))],
        out_specs=pl.BlockSpec((tm, LANES), lambda i: (i, 0)),
        out_shape=jax.ShapeDtypeStruct((n, LANES), F32),
        scratch_shapes=[pltpu.VMEM((1, LANES), F32)],
        compiler_params=_cparams(1),
        name="fox_cum",
    )(zs, bf_row, tri)


def _fox_pack_kernel(fq_ref, fk_ref, cum_ref, e_ref, q_ref, k_ref):
    c3 = jnp.concatenate(_split3(cum_ref[...] * LOG2E), axis=1)
    kx = jnp.dot(c3, e_ref[0], preferred_element_type=F32)
    lane = lax.broadcasted_iota(jnp.int32, (fq_ref.shape[0], LANES), 1)
    qaug = jnp.where(lane < 3, -1.0, 0.0).astype(q_ref.dtype)
    for hh in range(2):
        lo = hh * LANES
        q_ref[:, 2 * lo:2 * lo + LANES] = fq_ref[:, lo:lo + LANES]
        q_ref[:, 2 * lo + LANES:2 * lo + 2 * LANES] = qaug
        k_ref[:, 2 * lo:2 * lo + LANES] = fk_ref[:, lo:lo + LANES]
        k_ref[:, 2 * lo + LANES:2 * lo + 2 * LANES] = kx[:, 2 * lo + LANES:2 * lo + 2 * LANES].astype(k_ref.dtype)


def _fox_pack(zb, cum):
    n = zb.shape[0]
    tm = min(1024, n)
    pairs = FOX_HEADS // 2
    place = np.zeros((pairs, 3 * LANES, 4 * LANES), np.float32)
    for p in range(pairs):
        for hh in range(2):
            for part in range(3):
                place[p, part * LANES + 2 * p + hh, hh * 2 * LANES + LANES + part] = 1.0
    out = jax.ShapeDtypeStruct((n, FOX_HEADS * 2 * LANES), BF16)
    ospec = pl.BlockSpec((tm, 4 * LANES), lambda i, p: (i, p))
    return pl.pallas_call(
        _fox_pack_kernel,
        grid=(n // tm, pairs),
        in_specs=[pl.BlockSpec((tm, 2 * LANES), lambda i, p: (i, p)),
                  pl.BlockSpec((tm, 2 * LANES), lambda i, p: (i, pairs + p)),
                  pl.BlockSpec((tm, LANES), lambda i, p: (i, 0)),
                  pl.BlockSpec((1, 3 * LANES, 4 * LANES), lambda i, p: (p, 0, 0))],
        out_specs=[ospec, ospec],
        out_shape=[out, out],
        compiler_params=_cparams(2),
        name="fox_pack",
    )(zb, zb, cum, jnp.asarray(place, dtype=BF16))


def _attn_kernel(q_ref, k_ref, v_ref, o_ref, vt_sc, s_sc, m_sc, acc_sc, *, tc, nsub, dv):
    i = pl.program_id(1)
    nchunks, dva, _ = vt_sc.shape

    @pl.when(i == 0)
    def _():
        ones_row = (lax.broadcasted_iota(jnp.int32, (dva - dv, tc), 0) == 0).astype(vt_sc.dtype)

        def fill(c, carry):
            off = pl.multiple_of(c * tc, tc)
            vt_sc[c, 0:dv, :] = v_ref[pl.ds(off, tc), :].astype(F32).T.astype(vt_sc.dtype)
            vt_sc[c, dv:dva, :] = ones_row
            return carry

        lax.fori_loop(0, nchunks, fill, 0)

    m_sc[...] = jnp.full_like(m_sc, NEG)
    acc_sc[...] = jnp.zeros_like(acc_sc)

    def scores(j, rs, slot):
        off = pl.multiple_of(j * tc, tc)
        k = k_ref[pl.ds(off, tc), :]
        for r in rs:
            s_sc[slot, r] = lax.dot_general(k, q_ref[r * tc:(r + 1) * tc, :],
                                            (((1,), (1,)), ((), ())), preferred_element_type=F32)

    def update(j, rs, slot, masked_r):
        for r in rs:
            s = s_sc[slot, r]
            if r == masked_r:
                key = lax.broadcasted_iota(jnp.int32, (tc, tc), 0)
                qry = lax.broadcasted_iota(jnp.int32, (tc, tc), 1)
                s = jnp.where(key <= qry, s, NEG)
            m_prev = m_sc[r]
            m_new = jnp.maximum(m_prev, jnp.max(s, axis=0, keepdims=True))
            alpha = jnp.exp2(m_prev - m_new)
            p = jnp.exp2(s - m_new).astype(vt_sc.dtype)
            acc_sc[r] = alpha * acc_sc[r] + jnp.dot(vt_sc[j], p, preferred_element_type=F32)
            m_sc[r] = m_new

    all_r = range(nsub)
    n_main = i * nsub
    scores(0, all_r, 0)

    def body(t, carry):
        j = 2 * t
        scores(j + 1, all_r, 1)
        update(j, all_r, 0, -1)
        scores(j + 2, all_r, 0)
        update(j + 1, all_r, 1, -1)
        return carry

    lax.fori_loop(0, n_main // 2, body, 0)
    for jj in range(nsub):
        if jj + 1 < nsub:
            scores(n_main + jj + 1, range(jj + 1, nsub), (jj + 1) % 2)
        update(n_main + jj, range(jj, nsub), jj % 2, jj)
    for r in range(nsub):
        acc = acc_sc[r]
        o = acc[0:dv, :] / acc[dv:dv + 1, :]
        o_ref[r * tc:(r + 1) * tc, :] = o.T.astype(o_ref.dtype)


def _attention(q_arr, k_arr, v_arr, heads, dk, dv, q_blk0, k_blk0, v_blk0, name):
    n = q_arr.shape[0]
    tc = min(ATTN_CHUNK, n)
    nsub = min(ATTN_SUBBLOCKS, n // tc)
    tq = tc * nsub
    assert nsub % 2 == 0 and n % tq == 0
    dva = dv + BF16_SUBLANES
    return pl.pallas_call(
        functools.partial(_attn_kernel, tc=tc, nsub=nsub, dv=dv),
        grid=(heads, n // tq),
        in_specs=[pl.BlockSpec((tq, dk), lambda h, i: (i, q_blk0 + h)),
                  pl.BlockSpec((n, dk), lambda h, i: (0, k_blk0 + h)),
                  pl.BlockSpec((n, dv), lambda h, i: (0, v_blk0 + h))],
        out_specs=pl.BlockSpec((tq, dv), lambda h, i: (i, h)),
        out_shape=jax.ShapeDtypeStruct((n, heads * dv), BF16),
        scratch_shapes=[pltpu.VMEM((n // tc, dva, tc), BF16),
                        pltpu.VMEM((2, nsub, tc, tc), F32),
                        pltpu.VMEM((nsub, 1, tc), F32),
                        pltpu.VMEM((nsub, dva, tc), F32)],
        compiler_params=_cparams(2),
        name=name,
    )(q_arr, k_arr, v_arr)


def _merge_kernel(ya_ref, yb_ref, wa_ref, wb_ref, ga_ref, gb_ref, o_ref):
    a = jnp.dot(ya_ref[...], wa_ref[...], preferred_element_type=F32)
    b = jnp.dot(yb_ref[...], wb_ref[...], preferred_element_type=F32)
    ga = jax.nn.sigmoid(ga_ref[...].astype(F32))
    gb = jax.nn.sigmoid(gb_ref[...].astype(F32))
    o_ref[...] = (ga * a + gb * b).astype(o_ref.dtype)


def _merge(ya, yb, w_ya, w_yb, zb, ga_col0, gb_col0, layer):
    n, ka = ya.shape
    kb = yb.shape[1]
    d = w_ya.shape[-1]
    tm, tn = min(1024, n), 512
    ga0, gb0 = ga_col0 // tn, gb_col0 // tn
    return pl.pallas_call(
        _merge_kernel,
        grid=(n // tm, d // tn),
        in_specs=[pl.BlockSpec((tm, ka), lambda i, j: (i, 0)),
                  pl.BlockSpec((tm, kb), lambda i, j: (i, 0)),
                  _wspec(layer, (ka, tn), lambda i, j: (0, j)),
                  _wspec(layer, (kb, tn), lambda i, j: (0, j)),
                  pl.BlockSpec((tm, tn), lambda i, j: (i, ga0 + j)),
                  pl.BlockSpec((tm, tn), lambda i, j: (i, gb0 + j))],
        out_specs=pl.BlockSpec((tm, tn), lambda i, j: (i, j)),
        out_shape=jax.ShapeDtypeStruct((n, d), BF16),
        compiler_params=_cparams(2),
        name="merge",
    )(ya, yb, w_ya, w_yb, zb, zb)


def _ln_kernel(*refs, alpha, n_add, with_t):
    h_ref = refs[0]
    add_refs = refs[1:1 + n_add]
    g_ref, b_ref, o_ref, ob_ref = refs[1 + n_add:5 + n_add]
    x = alpha * h_ref[...]
    for r in add_refs:
        x = x + r[...].astype(F32)
    mu = jnp.mean(x, axis=-1, keepdims=True)
    xc = x - mu
    var = jnp.mean(xc * xc, axis=-1, keepdims=True)
    y = xc * lax.rsqrt(var + LN_EPS) * g_ref[...] + b_ref[...]
    o_ref[...] = y
    ob_ref[...] = y.astype(ob_ref.dtype)
    if with_t:
        obt_ref = refs[5 + n_add]
        obt_ref[...] = y.T.astype(obt_ref.dtype)


def _ln(h, addends, g, b, alpha, with_t):
    n, d = h.shape
    tm = min(256, n)
    row = pl.BlockSpec((tm, d), lambda i: (i, 0))
    vec = pl.BlockSpec((1, d), lambda i: (0, 0))
    out_specs = [row, row]
    out_shape = [jax.ShapeDtypeStruct((n, d), F32), jax.ShapeDtypeStruct((n, d), BF16)]
    if with_t:
        out_specs.append(pl.BlockSpec((d, tm), lambda i: (0, i)))
        out_shape.append(jax.ShapeDtypeStruct((d, n), BF16))
    return pl.pallas_call(
        functools.partial(_ln_kernel, alpha=alpha, n_add=len(addends), with_t=with_t),
        grid=(n // tm,),
        in_specs=[row] * (1 + len(addends)) + [vec, vec],
        out_specs=out_specs,
        out_shape=out_shape,
        compiler_params=_cparams(1),
        name="deepnorm_ln",
    )(h, *addends, g, b)


def _topk_rows(problems, k, pos=None):
    r, t = problems[0][0].shape
    iota = lax.broadcasted_iota(jnp.int32, (r, t), 0).astype(F32) if pos is None else pos
    for _, _, _, rank_ref in problems:
        if rank_ref is not None:
            rank_ref[...] = jnp.full((r, t), float(k), F32)

    def body(kk, carry):
        for x_ref, val_ref, idx_ref, rank_ref in problems:
            x = x_ref[...]
            m = jnp.max(x, axis=0, keepdims=True)
            idx = jnp.min(jnp.where(x == m, iota, POS_SENTINEL), axis=0, keepdims=True)
            sel = iota == idx
            x_ref[...] = jnp.where(sel, -jnp.inf, x)
            if rank_ref is not None:
                rank_ref[...] = jnp.where(sel, lax.convert_element_type(kk, F32), rank_ref[...])
            val_ref[pl.ds(kk, 1), :] = m
            idx_ref[pl.ds(kk, 1), :] = idx
        return carry

    lax.fori_loop(0, k, body, 0)


def _candidate_groups(k):
    assert k == 16
    return [(0, 16), (1, 8), (2, 8), (3, 8), (4, 4), (5, 4), (6, 4), (7, 4)] + [(a, 1) for a in range(8, 16)]


def _route_kernel(qp_ref, k1_ref, k2_ref, cpos_ref, nofi_ref, c_ref, r2_ref, e2_ref,
                  s_sc, cand_sc, t1_sc, i1_sc, t2_sc, i2_sc, top_sc, pos_sc, r1_sc, r2_sc):
    kk = PEER_TOPK
    qp = qp_ref[...].astype(BF16)
    nt = (((1,), (1,)), ((), ()))
    s1 = lax.dot_general(k1_ref[0].astype(BF16), qp[:, :PEER_HALF], nt, preferred_element_type=F32)
    s2 = lax.dot_general(k2_ref[0].astype(BF16), qp[:, PEER_HALF:], nt, preferred_element_type=F32)

    s_sc[0] = s1
    s_sc[1] = s2
    _topk_rows([(s_sc.at[0], t1_sc, i1_sc, r1_sc), (s_sc.at[1], t2_sc, i2_sc, r2_sc)], kk)

    row = 0
    for a, nb in _candidate_groups(kk):
        cand_sc[row:row + nb, :] = t1_sc[a:a + 1, :] + t2_sc[0:nb, :]
        row += nb
    _topk_rows([(cand_sc, top_sc, pos_sc, None)], kk, pos=cpos_ref[...])

    top = top_sc[...]
    z = jnp.sum(jnp.exp(top - top[0:1, :]), axis=0, keepdims=True)
    a_sel = jnp.floor(pos_sc[...] * (1.0 / kk))
    r1 = r1_sc[...]
    nofi = jnp.zeros_like(r1)
    for a in range(kk):
        n_a = jnp.sum((a_sel == float(a)).astype(F32), axis=0, keepdims=True)
        nofi = nofi + jnp.where(r1 == float(a), n_a, 0.0)
    nofi_ref[0] = nofi
    c_ref[0] = jnp.exp(s1 - t1_sc[0:1, :]) / z
    r2_ref[0] = r2_sc[...].astype(r2_ref.dtype)
    e2_ref[0] = jnp.exp(s2 - t2_sc[0:1, :]).astype(e2_ref.dtype)


def _route(qp, sub_k1, sub_k2):
    n = qp.shape[0]
    tr = min(512, n)
    kk = PEER_TOPK
    out = jax.ShapeDtypeStruct((PEER_HEADS, PEER_KEYS, n), F32)
    out_b = jax.ShapeDtypeStruct((PEER_HEADS, PEER_KEYS, n), BF16)
    ospec = pl.BlockSpec((1, PEER_KEYS, tr), lambda i, h: (h, 0, i))
    kspec = pl.BlockSpec((1, PEER_KEYS, PEER_HALF), lambda i, h: (h, 0, 0))
    small = pltpu.VMEM((kk, tr), F32)
    cpos = np.concatenate([a * kk + np.arange(nb) for a, nb in _candidate_groups(kk)]).astype(np.float32)
    ncand = cpos.shape[0]
    cpos = jnp.asarray(np.broadcast_to(cpos[:, None], (ncand, tr)))
    return pl.pallas_call(
        _route_kernel,
        grid=(n // tr, PEER_HEADS),
        in_specs=[pl.BlockSpec((tr, 2 * PEER_HALF), lambda i, h: (i, h)), kspec, kspec,
                  pl.BlockSpec((ncand, tr), lambda i, h: (0, 0))],
        out_specs=[ospec] * 4,
        out_shape=[out, out, out_b, out_b],
        scratch_shapes=[pltpu.VMEM((2, PEER_KEYS, tr), F32), pltpu.VMEM((ncand, tr), F32),
                        small, small, small, small, small, small,
                        pltpu.VMEM((PEER_KEYS, tr), F32), pltpu.VMEM((PEER_KEYS, tr), F32)],
        compiler_params=_cparams(2),
        name="peer_route",
    )(qp, sub_k1, sub_k2, cpos)


def _gelu_tanh(x):
    c = math.sqrt(2.0 / math.pi)
    return 0.5 * x * (1.0 + jnp.tanh(c * (x + 0.044715 * (x * x * x))))


def _gates(nofi_ref, c_ref, r2_ref, e2_ref, w_sc):
    for il in range(nofi_ref.shape[0]):
        w = None
        for h in range(PEER_HEADS):
            n_i = nofi_ref[il, h:h + 1, :].astype(BF16)
            c_i = c_ref[il, h:h + 1, :].astype(BF16)
            term = jnp.where(r2_ref[h] < n_i, e2_ref[h] * c_i, jnp.zeros((), BF16))
            w = term if w is None else w + term
        w_sc[il * PEER_KEYS:(il + 1) * PEER_KEYS, :] = w


def _expert_kernel(u_ref, vt_ref, ht_ref, nofi0_ref, c0_ref, nofi_ref, c_ref, r2_ref, e2_ref, o_ref,
                   acc_sc, g_sc, w_sc):
    e = pl.program_id(1)
    old = lax.rem(e + 1, 2)
    new = lax.rem(e, 2)

    @pl.when(e == 0)
    def _():
        acc_sc[...] = jnp.zeros_like(acc_sc)
        g_sc[...] = jnp.zeros_like(g_sc)
        _gates(nofi0_ref, c0_ref, r2_ref, e2_ref, w_sc.at[0])

    act = jnp.dot(u_ref[...], ht_ref[...], preferred_element_type=F32)
    acc_sc[...] += jnp.dot(vt_ref[...], g_sc[old], preferred_element_type=F32)
    g_sc[new] = _gelu_tanh(act.astype(g_sc.dtype)) * w_sc[new]
    _gates(nofi_ref, c_ref, r2_ref, e2_ref, w_sc.at[old])

    @pl.when(e == pl.num_programs(1) - 1)
    def _():
        o_ref[...] = acc_sc[...].T.astype(o_ref.dtype)


def _experts(u_b, vt_b, ht_b, nofi_t, c_t, r2, e2, layer=None):
    d, n = ht_b.shape
    ne = u_b.shape[-2]
    tm = min(512, n)
    te = 512
    gi = te // PEER_KEYS
    nblk = ne // te
    cur = lambda e: jnp.minimum(e, nblk - 1)
    nxt = lambda e: jnp.minimum(e + 1, nblk - 1)
    prev = lambda e: jnp.maximum(e - 1, 0)
    ispec0 = pl.BlockSpec((gi, PEER_HEADS, tm), lambda b, e: (0, 0, b))
    ispec = pl.BlockSpec((gi, PEER_HEADS, tm), lambda b, e: (nxt(e), 0, b))
    jspec = pl.BlockSpec((PEER_HEADS, PEER_KEYS, tm), lambda b, e: (0, 0, b))
    return pl.pallas_call(
        _expert_kernel,
        grid=(n // tm, nblk + 1),
        in_specs=[_wspec(layer, (te, d), lambda b, e: (cur(e), 0)),
                  _wspec(layer, (d, te), lambda b, e: (0, prev(e))),
                  pl.BlockSpec((d, tm), lambda b, e: (0, b)),
                  ispec0, ispec0, ispec, ispec, jspec, jspec],
        out_specs=pl.BlockSpec((tm, d), lambda b, e: (b, 0)),
        out_shape=jax.ShapeDtypeStruct((n, d), BF16),
        scratch_shapes=[pltpu.VMEM((d, tm), F32), pltpu.VMEM((2, te, tm), BF16),
                        pltpu.VMEM((2, te, tm), BF16)],
        compiler_params=_cparams(2),
        name="peer_experts",
    )(u_b, vt_b, ht_b, nofi_t, c_t, nofi_t, c_t, r2, e2)


def _ple_kernel(h_ref, wg_ref, p_ref, wp_ref, o_ref):
    g = jnp.dot(h_ref[...], wg_ref[...], preferred_element_type=F32)
    pp = jnp.dot(p_ref[...], wp_ref[...], preferred_element_type=F32)
    o_ref[...] = (jax.nn.sigmoid(g) * pp).astype(o_ref.dtype)


def _ple(h_b, w_pg, p_b, w_pe, layer):
    n, d = h_b.shape
    kp = p_b.shape[-1]
    tm, tn = min(1024, n), 1024
    return pl.pallas_call(
        _ple_kernel,
        grid=(n // tm, d // tn),
        in_specs=[pl.BlockSpec((tm, d), lambda i, j: (i, 0)),
                  _wspec(layer, (d, tn), lambda i, j: (0, j)),
                  _wspec(layer, (tm, kp), lambda i, j: (i, 0)),
                  _wspec(layer, (kp, tn), lambda i, j: (0, j))],
        out_specs=pl.BlockSpec((tm, tn), lambda i, j: (i, j)),
        out_shape=jax.ShapeDtypeStruct((n, d), BF16),
        compiler_params=_cparams(2),
        name="ple",
    )(h_b, w_pg, p_b, w_pe)


def _mixer_weights(w_in, w_uq, w_ukv, b_f):
    depth, d, _ = w_in.shape
    sizes = (MLA_Q_RANK, MLA_KV_RANK, MLA_ROPE, FOX_HEADS * FOX_DIM, FOX_HEADS * FOX_DIM,
             FOX_HEADS * FOX_DIM, FOX_HEADS, d, d)
    offs = np.concatenate([[0], np.cumsum(sizes)])
    cq, ckv, kr, _, _, _, fl, _, _ = [w_in[:, :, offs[t]:offs[t + 1]] for t in range(9)]
    zpad = lambda c: jnp.zeros((depth, d, c), w_in.dtype)
    w_small = jnp.concatenate([cq, ckv, kr, zpad(LANES - MLA_ROPE), fl, zpad(LANES - FOX_HEADS)],
                              axis=2).astype(BF16)
    w_fox = w_in[:, :, offs[3]:offs[6]].astype(BF16)
    w_gate = w_in[:, :, offs[7]:offs[9]].astype(BF16)
    colscale = np.ones((1, w_fox.shape[2]), np.float32)
    colscale[0, :FOX_HEADS * FOX_DIM] = FOX_DIM ** -0.5 * LOG2E
    qk = MLA_NOPE + MLA_ROPE
    wq3 = w_uq.reshape(depth, MLA_Q_RANK, MLA_HEADS, qk)
    wqn = wq3[..., :MLA_NOPE].reshape(depth, MLA_Q_RANK, MLA_HEADS * MLA_NOPE).astype(BF16)
    wqr = jnp.pad(wq3[..., MLA_NOPE:], ((0, 0), (0, 0), (0, 0), (0, LANES - MLA_ROPE))
                  ).reshape(depth, MLA_Q_RANK, MLA_HEADS * LANES).astype(BF16)
    wkv3 = w_ukv.reshape(depth, MLA_KV_RANK, MLA_HEADS, MLA_NOPE + MLA_V)
    wuk = wkv3[..., :MLA_NOPE].reshape(depth, MLA_KV_RANK, MLA_HEADS * MLA_NOPE).astype(BF16)
    wuv = wkv3[..., MLA_NOPE:].reshape(depth, MLA_KV_RANK, MLA_HEADS * MLA_V).astype(BF16)
    bf_rows = jnp.pad(b_f, ((0, 0), (0, LANES - FOX_HEADS))).reshape(depth, 1, LANES)
    return w_small, w_fox, w_gate, jnp.asarray(colscale), wqn, wqr, wuk, wuv, bf_rows


def kernel(x, p, positions, w_in, g_cq, g_ckv, w_uq, w_ukv, b_f, w_ya, w_yb, w_o, ln1_g, ln1_b,
           w_pq, sub_k1, sub_k2, u_tab, v_tab, w_pg, w_pe, ln2_g, ln2_b):
    bsz, seq, d = x.shape
    depth = w_in.shape[0]
    n = bsz * seq
    assert bsz == 1, "token-major layout assumes a single sequence"
    alpha = (2 * depth) ** 0.25

    h = x.reshape(n, d)
    h_b = h.astype(BF16)
    tabs = _rope_tables(positions.reshape(n, 1))

    w_small, w_fox, w_gate, colscale, wqn, wqr, wuk, wuv, bf_rows = _mixer_weights(w_in, w_uq, w_ukv, b_f)
    w_ya_b, w_yb_b, w_o_b = w_ya.astype(BF16), w_yb.astype(BF16), w_o.astype(BF16)
    w_pq_b, w_pg_b, w_pe_b = w_pq.astype(BF16), w_pg.astype(BF16), w_pe.astype(BF16)
    u_b = u_tab.astype(BF16)
    vt_b = v_tab.transpose(0, 2, 1).astype(BF16)
    p_b = p.reshape(depth, n, -1).astype(BF16)

    for i in range(depth):
        zs = _mm(h_b, w_small, F32, 1024, 256, name="in_proj_small", layer=i)
        zf = _mm(h_b, w_fox, BF16, 1024, 1024, colscale, name="in_proj_fox", layer=i)
        zg = _mm(h_b, w_gate, BF16, 1024, 1024, name="in_proj_gate", layer=i)
        q_mla = _mla_q(zs, g_cq[i].reshape(1, -1), wqn[i], wqr[i], tabs)
        k_mla, v_mla = _mla_kv(zs, g_ckv[i].reshape(1, -1), wuk[i], wuv[i], tabs)
        y_a = _attention(q_mla, k_mla, v_mla, MLA_HEADS, 2 * LANES, MLA_V, 0, 0, 0, "attn_mla")
        q_fox, k_fox = _fox_pack(zf, _fox_cum(zs, bf_rows[i]))
        y_b = _attention(q_fox, k_fox, zf, FOX_HEADS, 2 * LANES, FOX_DIM, 0, 0, 2 * FOX_HEADS,
                         "attn_fox")
        merged = _merge(y_a, y_b, w_ya_b, w_yb_b, zg, 0, d, i)
        mix = _mm(merged, w_o_b, BF16, 1024, 1024, name="out_proj", layer=i)
        h, h_b, ht_b = _ln(h, [mix], ln1_g[i].reshape(1, d), ln1_b[i].reshape(1, d), alpha, True)

        qp = _mm(h_b, w_pq_b, F32, 1024, 1024, name="peer_query", layer=i)
        nofi, c, r2, e2 = _route(qp, sub_k1[i], sub_k2[i])
        ffn = _experts(u_b, vt_b, ht_b, nofi.transpose(1, 0, 2), c.transpose(1, 0, 2), r2, e2, i)
        ple = _ple(h_b, w_pg_b, p_b, w_pe_b, i)
        h, h_b = _ln(h, [ffn, ple], ln2_g[i].reshape(1, d), ln2_b[i].reshape(1, d), alpha, False)

    return h.reshape(bsz, seq, d)
```

```python
import functools
import math

import numpy as np
import jax
import jax.numpy as jnp
from jax import lax
from jax.experimental import pallas as pl
from jax.experimental.pallas import tpu as pltpu

F32 = jnp.float32
BF16 = jnp.bfloat16
F8 = jnp.float8_e4m3fn
F8_TARGET_MAX = 240.0

MLA_HEADS = 16
MLA_Q_RANK = 1024
MLA_KV_RANK = 512
MLA_NOPE = 128
MLA_ROPE = 64
MLA_V = 128
ROPE_THETA = 10000.0
FOX_HEADS = 16
FOX_DIM = 128
PEER_HEADS = 8
PEER_KEYS = 128
PEER_HALF = 128
PEER_TOPK = 16
LN_EPS = 1e-5
RMS_EPS = 1e-6
LANES = 128
BF16_SUBLANES = 16
LOG2E = math.log2(math.e)
NEG = -1e30
POS_SENTINEL = 1e9
MLA_HEAD_GROUP = 4
ATTN_CHUNK = 1024
ATTN_SUBBLOCKS = 2

VMEM_LIMIT = 56 * 1024 * 1024


def _cparams(n_axes):
    return pltpu.CompilerParams(dimension_semantics=("arbitrary",) * n_axes,
                                vmem_limit_bytes=VMEM_LIMIT)


def _mm_kernel(x_ref, w_ref, o_ref):
    o_ref[...] = jnp.dot(x_ref[...], w_ref[...], preferred_element_type=F32).astype(o_ref.dtype)


def _mm_scale_kernel(x_ref, w_ref, s_ref, o_ref):
    acc = jnp.dot(x_ref[...], w_ref[...], preferred_element_type=F32)
    o_ref[...] = (acc * s_ref[...]).astype(o_ref.dtype)


def _wspec(layer, block, index_map):
    if layer is None:
        return pl.BlockSpec(block, index_map)
    return pl.BlockSpec((None,) + block, lambda *g: (layer,) + index_map(*g))


def _mm(x, w, out_dtype, tm, tn, colscale=None, name="mm", layer=None):
    m, k = x.shape
    n = w.shape[-1]
    tm, tn = min(tm, m), min(tn, n)
    assert m % tm == 0 and n % tn == 0
    in_specs = [pl.BlockSpec((tm, k), lambda i, j: (i, 0)),
                _wspec(layer, (k, tn), lambda i, j: (0, j))]
    args = [x, w]
    body = _mm_kernel
    if colscale is not None:
        in_specs.append(pl.BlockSpec((1, tn), lambda i, j: (0, j)))
        args.append(colscale)
        body = _mm_scale_kernel
    return pl.pallas_call(
        body,
        grid=(m // tm, n // tn),
        in_specs=in_specs,
        out_specs=pl.BlockSpec((tm, tn), lambda i, j: (i, j)),
        out_shape=jax.ShapeDtypeStruct((m, n), out_dtype),
        compiler_params=_cparams(2),
        name=name,
    )(*args)


def _rope_table_kernel(pos_ref, c_ref, cos_ref, sina_ref, sinb_ref):
    ang = pos_ref[...].astype(F32) * c_ref[0:1, :]
    cos = jnp.cos(ang)
    sin = jnp.sin(ang)
    cos_ref[...] = cos * c_ref[1:2, :]
    sina_ref[...] = sin * c_ref[2:3, :]
    sinb_ref[...] = sin * c_ref[3:4, :]


def _rope_tables(positions_col):
    n = positions_col.shape[0]
    tm = min(1024, n)
    half = MLA_ROPE // 2
    inv_freq = (ROPE_THETA ** (-np.arange(half, dtype=np.float32) / half)).astype(np.float32)
    consts = np.zeros((8, LANES), np.float32)
    consts[0, :half] = inv_freq
    consts[0, half:2 * half] = inv_freq
    consts[1, :2 * half] = 1.0
    consts[2, :half] = -1.0
    consts[3, half:2 * half] = 1.0
    shp = jax.ShapeDtypeStruct((n, LANES), F32)
    return pl.pallas_call(
        _rope_table_kernel,
        grid=(n // tm,),
        in_specs=[pl.BlockSpec((tm, 1), lambda i: (i, 0)),
                  pl.BlockSpec((8, LANES), lambda i: (0, 0))],
        out_specs=[pl.BlockSpec((tm, LANES), lambda i: (i, 0))] * 3,
        out_shape=[shp, shp, shp],
        compiler_params=_cparams(1),
        name="rope_tables",
    )(positions_col, jnp.asarray(consts))


def _rope128(x, cos, sina, sinb):
    return (x * cos + pltpu.roll(x, 3 * (MLA_ROPE // 2), 1) * sina
            + pltpu.roll(x, MLA_ROPE // 2, 1) * sinb)


def _rms(x, g):
    return x * lax.rsqrt(jnp.mean(x * x, axis=-1, keepdims=True) + RMS_EPS) * g


def _mla_q_kernel(cq_ref, g_ref, wn_ref, wr_ref, cos_ref, sina_ref, sinb_ref, q_ref, *, scale):
    xn = _rms(cq_ref[...], g_ref[...]).astype(BF16)
    qn = jnp.dot(xn, wn_ref[...], preferred_element_type=F32) * scale
    qr = jnp.dot(xn, wr_ref[...], preferred_element_type=F32) * scale
    cos, sina, sinb = cos_ref[...], sina_ref[...], sinb_ref[...]
    for hh in range(wn_ref.shape[1] // LANES):
        lo = hh * LANES
        q_ref[:, 2 * lo:2 * lo + LANES] = qn[:, lo:lo + LANES].astype(q_ref.dtype)
        q_ref[:, 2 * lo + LANES:2 * lo + 2 * LANES] = _rope128(
            qr[:, lo:lo + LANES], cos, sina, sinb).astype(q_ref.dtype)


def _mla_q(zs, g_cq, wqn, wqr, tabs):
    n = zs.shape[0]
    tm = min(1024, n)
    grp = MLA_HEAD_GROUP
    scale = (MLA_NOPE + MLA_ROPE) ** -0.5 * LOG2E
    tab_spec = pl.BlockSpec((tm, LANES), lambda i, p: (i, 0))
    return pl.pallas_call(
        functools.partial(_mla_q_kernel, scale=scale),
        grid=(n // tm, MLA_HEADS // grp),
        in_specs=[pl.BlockSpec((tm, MLA_Q_RANK), lambda i, p: (i, 0)),
                  pl.BlockSpec((1, MLA_Q_RANK), lambda i, p: (0, 0)),
                  pl.BlockSpec((MLA_Q_RANK, grp * LANES), lambda i, p: (0, p)),
                  pl.BlockSpec((MLA_Q_RANK, grp * LANES), lambda i, p: (0, p)),
                  tab_spec, tab_spec, tab_spec],
        out_specs=pl.BlockSpec((tm, grp * 2 * LANES), lambda i, p: (i, p)),
        out_shape=jax.ShapeDtypeStruct((n, MLA_HEADS * 2 * LANES), BF16),
        compiler_params=_cparams(2),
        name="mla_q",
    )(zs, g_cq, wqn, wqr, *tabs)


def _mla_kv_kernel(ckv_ref, g_ref, kr_ref, wk_ref, wv_ref, cos_ref, sina_ref, sinb_ref,
                   k_ref, v_ref):
    xn = _rms(ckv_ref[...], g_ref[...]).astype(BF16)
    kn = jnp.dot(xn, wk_ref[...], preferred_element_type=F32)
    v_ref[...] = jnp.dot(xn, wv_ref[...], preferred_element_type=F32).astype(v_ref.dtype)
    kr = _rope128(kr_ref[...], cos_ref[...], sina_ref[...], sinb_ref[...]).astype(k_ref.dtype)
    for hh in range(wk_ref.shape[1] // LANES):
        lo = hh * LANES
        k_ref[:, 2 * lo:2 * lo + LANES] = kn[:, lo:lo + LANES].astype(k_ref.dtype)
        k_ref[:, 2 * lo + LANES:2 * lo + 2 * LANES] = kr


def _mla_kv(zs, g_ckv, wuk, wuv, tabs):
    n = zs.shape[0]
    tm = min(1024, n)
    grp = MLA_HEAD_GROUP
    ckv_blk = MLA_Q_RANK // MLA_KV_RANK
    kr_blk = (MLA_Q_RANK + MLA_KV_RANK) // LANES
    tab_spec = pl.BlockSpec((tm, LANES), lambda i, p: (i, 0))
    return pl.pallas_call(
        _mla_kv_kernel,
        grid=(n // tm, MLA_HEADS // grp),
        in_specs=[pl.BlockSpec((tm, MLA_KV_RANK), lambda i, p: (i, ckv_blk)),
                  pl.BlockSpec((1, MLA_KV_RANK), lambda i, p: (0, 0)),
                  pl.BlockSpec((tm, LANES), lambda i, p: (i, kr_blk)),
                  pl.BlockSpec((MLA_KV_RANK, grp * LANES), lambda i, p: (0, p)),
                  pl.BlockSpec((MLA_KV_RANK, grp * LANES), lambda i, p: (0, p)),
                  tab_spec, tab_spec, tab_spec],
        out_specs=[pl.BlockSpec((tm, grp * 2 * LANES), lambda i, p: (i, p)),
                   pl.BlockSpec((tm, grp * LANES), lambda i, p: (i, p))],
        out_shape=[jax.ShapeDtypeStruct((n, MLA_HEADS * 2 * LANES), BF16),
                   jax.ShapeDtypeStruct((n, MLA_HEADS * MLA_V), BF16)],
        compiler_params=_cparams(2),
        name="mla_kv",
    )(zs, g_ckv, zs, wuk, wuv, *tabs)


def _split3(x):
    hi = x.astype(BF16)
    r = x - hi.astype(F32)
    mid = r.astype(BF16)
    lo = (r - mid.astype(F32)).astype(BF16)
    return hi, mid, lo


def _cum_kernel(fl_ref, bf_ref, tri_ref, o_ref, carry_ref):
    @pl.when(pl.program_id(0) == 0)
    def _():
        carry_ref[...] = jnp.zeros_like(carry_ref)

    x = fl_ref[...] + bf_ref[...]
    lf = jnp.minimum(x, 0.0) - jnp.log1p(jnp.exp(-jnp.abs(x)))
    tri = tri_ref[...]
    c = carry_ref[...]
    for part in _split3(lf):
        c = c + jnp.dot(tri, part, preferred_element_type=F32)
    carry_ref[...] = c[c.shape[0] - 1:, :]
    o_ref[...] = c


def _fox_cum(zs, bf_row):
    n = zs.shape[0]
    tm = 256
    fl_blk = (MLA_Q_RANK + MLA_KV_RANK) // LANES + 1
    tri = jnp.asarray(np.tril(np.ones((tm, tm), np.float32)), dtype=BF16)
    return pl.pallas_call(
        _cum_kernel,
        grid=(n // tm,),
        in_specs=[pl.BlockSpec((tm, LANES), lambda i: (i, fl_blk)),
                  pl.BlockSpec((1, LANES), lambda i: (0, 0)),
                  pl.BlockSpec((tm, tm), lambda i: (0, 0))],
        out_specs=pl.BlockSpec((tm, LANES), lambda i: (i, 0)),
        out_shape=jax.ShapeDtypeStruct((n, LANES), F32),
        scratch_shapes=[pltpu.VMEM((1, LANES), F32)],
        compiler_params=_cparams(1),
        name="fox_cum",
    )(zs, bf_row, tri)


def _fox_pack_kernel(fq_ref, fk_ref, cum_ref, e_ref, q_ref, k_ref):
    c3 = jnp.concatenate(_split3(cum_ref[...] * LOG2E), axis=1)
    kx = jnp.dot(c3, e_ref[0], preferred_element_type=F32)
    lane = lax.broadcasted_iota(jnp.int32, (fq_ref.shape[0], LANES), 1)
    qaug = jnp.where(lane < 3, -1.0, 0.0).astype(q_ref.dtype)
    for hh in range(2):
        lo = hh * LANES
        q_ref[:, 2 * lo:2 * lo + LANES] = fq_ref[:, lo:lo + LANES]
        q_ref[:, 2 * lo + LANES:2 * lo + 2 * LANES] = qaug
        k_ref[:, 2 * lo:2 * lo + LANES] = fk_ref[:, lo:lo + LANES]
        k_ref[:, 2 * lo + LANES:2 * lo + 2 * LANES] = kx[:, 2 * lo + LANES:2 * lo + 2 * LANES].astype(k_ref.dtype)


def _fox_pack(zb, cum):
    n = zb.shape[0]
    tm = min(1024, n)
    pairs = FOX_HEADS // 2
    place = np.zeros((pairs, 3 * LANES, 4 * LANES), np.float32)
    for p in range(pairs):
        for hh in range(2):
            for part in range(3):
                place[p, part * LANES + 2 * p + hh, hh * 2 * LANES + LANES + part] = 1.0
    out = jax.ShapeDtypeStruct((n, FOX_HEADS * 2 * LANES), BF16)
    ospec = pl.BlockSpec((tm, 4 * LANES), lambda i, p: (i, p))
    return pl.pallas_call(
        _fox_pack_kernel,
        grid=(n // tm, pairs),
        in_specs=[pl.BlockSpec((tm, 2 * LANES), lambda i, p: (i, p)),
                  pl.BlockSpec((tm, 2 * LANES), lambda i, p: (i, pairs + p)),
                  pl.BlockSpec((tm, LANES), lambda i, p: (i, 0)),
                  pl.BlockSpec((1, 3 * LANES, 4 * LANES), lambda i, p: (p, 0, 0))],
        out_specs=[ospec, ospec],
        out_shape=[out, out],
        compiler_params=_cparams(2),
        name="fox_pack",
    )(zb, zb, cum, jnp.asarray(place, dtype=BF16))


def _attn_kernel(q_ref, k_ref, v_ref, o_ref, vt_sc, s_sc, m_sc, acc_sc, *, tc, nsub, dv):
    i = pl.program_id(1)
    nchunks, dva, _ = vt_sc.shape

    @pl.when(i == 0)
    def _():
        ones_row = (lax.broadcasted_iota(jnp.int32, (dva - dv, tc), 0) == 0).astype(vt_sc.dtype)

        def fill(c, carry):
            off = pl.multiple_of(c * tc, tc)
            vt_sc[c, 0:dv, :] = v_ref[pl.ds(off, tc), :].astype(F32).T.astype(vt_sc.dtype)
            vt_sc[c, dv:dva, :] = ones_row
            return carry

        lax.fori_loop(0, nchunks, fill, 0)

    m_sc[...] = jnp.full_like(m_sc, NEG)
    acc_sc[...] = jnp.zeros_like(acc_sc)

    def scores(j, rs, slot):
        off = pl.multiple_of(j * tc, tc)
        k = k_ref[pl.ds(off, tc), :]
        for r in rs:
            s_sc[slot, r] = lax.dot_general(k, q_ref[r * tc:(r + 1) * tc, :],
                                            (((1,), (1,)), ((), ())), preferred_element_type=F32)

    def update(j, rs, slot, masked_r):
        for r in rs:
            s = s_sc[slot, r]
            if r == masked_r:
                key = lax.broadcasted_iota(jnp.int32, (tc, tc), 0)
                qry = lax.broadcasted_iota(jnp.int32, (tc, tc), 1)
                s = jnp.where(key <= qry, s, NEG)
            m_prev = m_sc[r]
            m_new = jnp.maximum(m_prev, jnp.max(s, axis=0, keepdims=True))
            alpha = jnp.exp2(m_prev - m_new)
            p = jnp.exp2(s - m_new).astype(vt_sc.dtype)
            acc_sc[r] = alpha * acc_sc[r] + jnp.dot(vt_sc[j], p, preferred_element_type=F32)
            m_sc[r] = m_new

    all_r = range(nsub)
    n_main = i * nsub
    scores(0, all_r, 0)

    def body(t, carry):
        j = 2 * t
        scores(j + 1, all_r, 1)
        update(j, all_r, 0, -1)
        scores(j + 2, all_r, 0)
        update(j + 1, all_r, 1, -1)
        return carry

    lax.fori_loop(0, n_main // 2, body, 0)
    for jj in range(nsub):
        if jj + 1 < nsub:
            scores(n_main + jj + 1, range(jj + 1, nsub), (jj + 1) % 2)
        update(n_main + jj, range(jj, nsub), jj % 2, jj)
    for r in range(nsub):
        acc = acc_sc[r]
        o = acc[0:dv, :] / acc[dv:dv + 1, :]
        o_ref[r * tc:(r + 1) * tc, :] = o.T.astype(o_ref.dtype)


def _attention(q_arr, k_arr, v_arr, heads, dk, dv, q_blk0, k_blk0, v_blk0, name):
    n = q_arr.shape[0]
    tc = min(ATTN_CHUNK, n)
    nsub = min(ATTN_SUBBLOCKS, n // tc)
    tq = tc * nsub
    assert nsub % 2 == 0 and n % tq == 0
    dva = dv + BF16_SUBLANES
    return pl.pallas_call(
        functools.partial(_attn_kernel, tc=tc, nsub=nsub, dv=dv),
        grid=(heads, n // tq),
        in_specs=[pl.BlockSpec((tq, dk), lambda h, i: (i, q_blk0 + h)),
                  pl.BlockSpec((n, dk), lambda h, i: (0, k_blk0 + h)),
                  pl.BlockSpec((n, dv), lambda h, i: (0, v_blk0 + h))],
        out_specs=pl.BlockSpec((tq, dv), lambda h, i: (i, h)),
        out_shape=jax.ShapeDtypeStruct((n, heads * dv), BF16),
        scratch_shapes=[pltpu.VMEM((n // tc, dva, tc), BF16),
                        pltpu.VMEM((2, nsub, tc, tc), F32),
                        pltpu.VMEM((nsub, 1, tc), F32),
                        pltpu.VMEM((nsub, dva, tc), F32)],
        compiler_params=_cparams(2),
        name=name,
    )(q_arr, k_arr, v_arr)


def _merge_kernel(ya_ref, yb_ref, wa_ref, wb_ref, ga_ref, gb_ref, o_ref):
    a = jnp.dot(ya_ref[...], wa_ref[...], preferred_element_type=F32)
    b = jnp.dot(yb_ref[...], wb_ref[...], preferred_element_type=F32)
    ga = jax.nn.sigmoid(ga_ref[...].astype(F32))
    gb = jax.nn.sigmoid(gb_ref[...].astype(F32))
    o_ref[...] = (ga * a + gb * b).astype(o_ref.dtype)


def _merge(ya, yb, w_ya, w_yb, zb, ga_col0, gb_col0, layer):
    n, ka = ya.shape
    kb = yb.shape[1]
    d = w_ya.shape[-1]
    tm, tn = min(1024, n), 512
    ga0, gb0 = ga_col0 // tn, gb_col0 // tn
    return pl.pallas_call(
        _merge_kernel,
        grid=(n // tm, d // tn),
        in_specs=[pl.BlockSpec((tm, ka), lambda i, j: (i, 0)),
                  pl.BlockSpec((tm, kb), lambda i, j: (i, 0)),
                  _wspec(layer, (ka, tn), lambda i, j: (0, j)),
                  _wspec(layer, (kb, tn), lambda i, j: (0, j)),
                  pl.BlockSpec((tm, tn), lambda i, j: (i, ga0 + j)),
                  pl.BlockSpec((tm, tn), lambda i, j: (i, gb0 + j))],
        out_specs=pl.BlockSpec((tm, tn), lambda i, j: (i, j)),
        out_shape=jax.ShapeDtypeStruct((n, d), BF16),
        compiler_params=_cparams(2),
        name="merge",
    )(ya, yb, w_ya, w_yb, zb, zb)


def _ln_kernel(*refs, alpha, n_add, with_t):
    h_ref = refs[0]
    add_refs = refs[1:1 + n_add]
    g_ref, b_ref = refs[1 + n_add:3 + n_add]
    outs = refs[3 + n_add + (1 if with_t else 0):]
    x = alpha * h_ref[...]
    for r in add_refs:
        x = x + r[...].astype(F32)
    mu = jnp.mean(x, axis=-1, keepdims=True)
    xc = x - mu
    var = jnp.mean(xc * xc, axis=-1, keepdims=True)
    y = xc * lax.rsqrt(var + LN_EPS) * g_ref[...] + b_ref[...]
    outs[0][...] = y
    outs[1][...] = y.astype(outs[1].dtype)
    if with_t:
        ts_ref = refs[3 + n_add]
        outs[2][...] = (y.T * ts_ref[...]).astype(outs[2].dtype)


def _ln(h, addends, g, b, alpha, t_scale=None):
    n, d = h.shape
    tm = min(256, n)
    with_t = t_scale is not None
    row = pl.BlockSpec((tm, d), lambda i: (i, 0))
    vec = pl.BlockSpec((1, d), lambda i: (0, 0))
    in_specs = [row] * (1 + len(addends)) + [vec, vec]
    args = [h, *addends, g, b]
    out_specs = [row, row]
    out_shape = [jax.ShapeDtypeStruct((n, d), F32), jax.ShapeDtypeStruct((n, d), BF16)]
    if with_t:
        in_specs.append(pl.BlockSpec((1, 1), lambda i: (0, 0)))
        args.append(t_scale)
        out_specs.append(pl.BlockSpec((d, tm), lambda i: (0, i)))
        out_shape.append(jax.ShapeDtypeStruct((d, n), F8))
    return pl.pallas_call(
        functools.partial(_ln_kernel, alpha=alpha, n_add=len(addends), with_t=with_t),
        grid=(n // tm,),
        in_specs=in_specs,
        out_specs=out_specs,
        out_shape=out_shape,
        compiler_params=_cparams(1),
        name="deepnorm_ln",
    )(*args)


def _topk_rows(problems, k, pos=None):
    r, t = problems[0][0].shape
    iota = lax.broadcasted_iota(jnp.int32, (r, t), 0).astype(F32) if pos is None else pos
    for _, _, _, rank_ref in problems:
        if rank_ref is not None:
            rank_ref[...] = jnp.full((r, t), float(k), F32)

    def body(kk, carry):
        for x_ref, val_ref, idx_ref, rank_ref in problems:
            x = x_ref[...]
            m = jnp.max(x, axis=0, keepdims=True)
            idx = jnp.min(jnp.where(x == m, iota, POS_SENTINEL), axis=0, keepdims=True)
            sel = iota == idx
            x_ref[...] = jnp.where(sel, -jnp.inf, x)
            if rank_ref is not None:
                rank_ref[...] = jnp.where(sel, lax.convert_element_type(kk, F32), rank_ref[...])
            val_ref[pl.ds(kk, 1), :] = m
            idx_ref[pl.ds(kk, 1), :] = idx
        return carry

    lax.fori_loop(0, k, body, 0)


def _candidate_groups(k):
    assert k == 16
    return [(0, 16), (1, 8), (2, 8), (3, 8), (4, 4), (5, 4), (6, 4), (7, 4)] + [(a, 1) for a in range(8, 16)]


def _route_kernel(qp_ref, k1_ref, k2_ref, cpos_ref, nofi_ref, c_ref, r2_ref, e2_ref,
                  s_sc, cand_sc, t1_sc, i1_sc, t2_sc, i2_sc, top_sc, pos_sc, r1_sc, r2_sc):
    kk = PEER_TOPK
    qp = qp_ref[...].astype(BF16)
    nt = (((1,), (1,)), ((), ()))
    s1 = lax.dot_general(k1_ref[0].astype(BF16), qp[:, :PEER_HALF], nt, preferred_element_type=F32)
    s2 = lax.dot_general(k2_ref[0].astype(BF16), qp[:, PEER_HALF:], nt, preferred_element_type=F32)

    s_sc[0] = s1
    s_sc[1] = s2
    _topk_rows([(s_sc.at[0], t1_sc, i1_sc, r1_sc), (s_sc.at[1], t2_sc, i2_sc, r2_sc)], kk)

    row = 0
    for a, nb in _candidate_groups(kk):
        cand_sc[row:row + nb, :] = t1_sc[a:a + 1, :] + t2_sc[0:nb, :]
        row += nb
    _topk_rows([(cand_sc, top_sc, pos_sc, None)], kk, pos=cpos_ref[...])

    top = top_sc[...]
    z = jnp.sum(jnp.exp(top - top[0:1, :]), axis=0, keepdims=True)
    a_sel = jnp.floor(pos_sc[...] * (1.0 / kk))
    r1 = r1_sc[...]
    nofi = jnp.zeros_like(r1)
    for a in range(kk):
        n_a = jnp.sum((a_sel == float(a)).astype(F32), axis=0, keepdims=True)
        nofi = nofi + jnp.where(r1 == float(a), n_a, 0.0)
    nofi_ref[0] = nofi
    c_ref[0] = jnp.exp(s1 - t1_sc[0:1, :]) / z
    r2_ref[0] = r2_sc[...].astype(r2_ref.dtype)
    e2_ref[0] = jnp.exp(s2 - t2_sc[0:1, :]).astype(e2_ref.dtype)


def _route(qp, sub_k1, sub_k2):
    n = qp.shape[0]
    tr = min(512, n)
    kk = PEER_TOPK
    out = jax.ShapeDtypeStruct((PEER_HEADS, PEER_KEYS, n), F32)
    out_b = jax.ShapeDtypeStruct((PEER_HEADS, PEER_KEYS, n), BF16)
    ospec = pl.BlockSpec((1, PEER_KEYS, tr), lambda i, h: (h, 0, i))
    kspec = pl.BlockSpec((1, PEER_KEYS, PEER_HALF), lambda i, h: (h, 0, 0))
    small = pltpu.VMEM((kk, tr), F32)
    cpos = np.concatenate([a * kk + np.arange(nb) for a, nb in _candidate_groups(kk)]).astype(np.float32)
    ncand = cpos.shape[0]
    cpos = jnp.asarray(np.broadcast_to(cpos[:, None], (ncand, tr)))
    return pl.pallas_call(
        _route_kernel,
        grid=(n // tr, PEER_HEADS),
        in_specs=[pl.BlockSpec((tr, 2 * PEER_HALF), lambda i, h: (i, h)), kspec, kspec,
                  pl.BlockSpec((ncand, tr), lambda i, h: (0, 0))],
        out_specs=[ospec] * 4,
        out_shape=[out, out, out_b, out_b],
        scratch_shapes=[pltpu.VMEM((2, PEER_KEYS, tr), F32), pltpu.VMEM((ncand, tr), F32),
                        small, small, small, small, small, small,
                        pltpu.VMEM((PEER_KEYS, tr), F32), pltpu.VMEM((PEER_KEYS, tr), F32)],
        compiler_params=_cparams(2),
        name="peer_route",
    )(qp, sub_k1, sub_k2, cpos)


def _gelu_tanh(x):
    c = math.sqrt(2.0 / math.pi)
    return 0.5 * x * (1.0 + jnp.tanh(c * (x + 0.044715 * (x * x * x))))


def _gates(nofi_ref, c_ref, r2_ref, e2_ref, w_sc):
    for il in range(nofi_ref.shape[0]):
        w = None
        for h in range(PEER_HEADS):
            n_i = nofi_ref[il, h:h + 1, :].astype(BF16)
            c_i = c_ref[il, h:h + 1, :].astype(BF16)
            term = jnp.where(r2_ref[h] < n_i, e2_ref[h] * c_i, jnp.zeros((), BF16))
            w = term if w is None else w + term
        w_sc[il * PEER_KEYS:(il + 1) * PEER_KEYS, :] = w


def _expert_kernel(u_ref, vt_ref, ht_ref, inv_ref, nofi0_ref, c0_ref, nofi_ref, c_ref, r2_ref, e2_ref,
                   o_ref, acc_sc, g_sc, w_sc):
    e = pl.program_id(1)
    old = lax.rem(e + 1, 2)
    new = lax.rem(e, 2)

    @pl.when(e == 0)
    def _():
        acc_sc[...] = jnp.zeros_like(acc_sc)
        g_sc[...] = jnp.zeros_like(g_sc)
        _gates(nofi0_ref, c0_ref, r2_ref, e2_ref, w_sc.at[0])

    act = jnp.dot(u_ref[...], ht_ref[...], preferred_element_type=F32) * inv_ref[...]
    acc_sc[...] += jnp.dot(vt_ref[...], g_sc[old], preferred_element_type=F32)
    g_sc[new] = _gelu_tanh(act.astype(g_sc.dtype)) * w_sc[new]
    _gates(nofi_ref, c_ref, r2_ref, e2_ref, w_sc.at[old])

    @pl.when(e == pl.num_programs(1) - 1)
    def _():
        o_ref[...] = acc_sc[...].T.astype(o_ref.dtype)


def _experts(u_q, vt_b, ht_q, inv_scale, nofi_t, c_t, r2, e2, layer=None):
    d, n = ht_q.shape
    ne = u_q.shape[-2]
    tm = min(512, n)
    te = 512
    gi = te // PEER_KEYS
    nblk = ne // te
    cur = lambda e: jnp.minimum(e, nblk - 1)
    nxt = lambda e: jnp.minimum(e + 1, nblk - 1)
    prev = lambda e: jnp.maximum(e - 1, 0)
    ispec0 = pl.BlockSpec((gi, PEER_HEADS, tm), lambda b, e: (0, 0, b))
    ispec = pl.BlockSpec((gi, PEER_HEADS, tm), lambda b, e: (nxt(e), 0, b))
    jspec = pl.BlockSpec((PEER_HEADS, PEER_KEYS, tm), lambda b, e: (0, 0, b))
    return pl.pallas_call(
        _expert_kernel,
        grid=(n // tm, nblk + 1),
        in_specs=[_wspec(layer, (te, d), lambda b, e: (cur(e), 0)),
                  _wspec(layer, (d, te), lambda b, e: (0, prev(e))),
                  pl.BlockSpec((d, tm), lambda b, e: (0, b)),
                  pl.BlockSpec((1, 1), lambda b, e: (0, 0)),
                  ispec0, ispec0, ispec, ispec, jspec, jspec],
        out_specs=pl.BlockSpec((tm, d), lambda b, e: (b, 0)),
        out_shape=jax.ShapeDtypeStruct((n, d), BF16),
        scratch_shapes=[pltpu.VMEM((d, tm), F32), pltpu.VMEM((2, te, tm), BF16),
                        pltpu.VMEM((2, te, tm), BF16)],
        compiler_params=_cparams(2),
        name="peer_experts",
    )(u_q, vt_b, ht_q, inv_scale, nofi_t, c_t, nofi_t, c_t, r2, e2)


def _ple_kernel(h_ref, wg_ref, p_ref, wp_ref, o_ref):
    g = jnp.dot(h_ref[...], wg_ref[...], preferred_element_type=F32)
    pp = jnp.dot(p_ref[...], wp_ref[...], preferred_element_type=F32)
    o_ref[...] = (jax.nn.sigmoid(g) * pp).astype(o_ref.dtype)


def _ple(h_b, w_pg, p_b, w_pe, layer):
    n, d = h_b.shape
    kp = p_b.shape[-1]
    tm, tn = min(1024, n), 1024
    return pl.pallas_call(
        _ple_kernel,
        grid=(n // tm, d // tn),
        in_specs=[pl.BlockSpec((tm, d), lambda i, j: (i, 0)),
                  _wspec(layer, (d, tn), lambda i, j: (0, j)),
                  _wspec(layer, (tm, kp), lambda i, j: (i, 0)),
                  _wspec(layer, (kp, tn), lambda i, j: (0, j))],
        out_specs=pl.BlockSpec((tm, tn), lambda i, j: (i, j)),
        out_shape=jax.ShapeDtypeStruct((n, d), BF16),
        compiler_params=_cparams(2),
        name="ple",
    )(h_b, w_pg, p_b, w_pe)


def _mixer_weights(w_in, w_uq, w_ukv, b_f):
    depth, d, _ = w_in.shape
    sizes = (MLA_Q_RANK, MLA_KV_RANK, MLA_ROPE, FOX_HEADS * FOX_DIM, FOX_HEADS * FOX_DIM,
             FOX_HEADS * FOX_DIM, FOX_HEADS, d, d)
    offs = np.concatenate([[0], np.cumsum(sizes)])
    cq, ckv, kr, _, _, _, fl, _, _ = [w_in[:, :, offs[t]:offs[t + 1]] for t in range(9)]
    zpad = lambda c: jnp.zeros((depth, d, c), w_in.dtype)
    w_small = jnp.concatenate([cq, ckv, kr, zpad(LANES - MLA_ROPE), fl, zpad(LANES - FOX_HEADS)],
                              axis=2).astype(BF16)
    w_fox = w_in[:, :, offs[3]:offs[6]].astype(BF16)
    w_gate = w_in[:, :, offs[7]:offs[9]].astype(BF16)
    colscale = np.ones((1, w_fox.shape[2]), np.float32)
    colscale[0, :FOX_HEADS * FOX_DIM] = FOX_DIM ** -0.5 * LOG2E
    qk = MLA_NOPE + MLA_ROPE
    wq3 = w_uq.reshape(depth, MLA_Q_RANK, MLA_HEADS, qk)
    wqn = wq3[..., :MLA_NOPE].reshape(depth, MLA_Q_RANK, MLA_HEADS * MLA_NOPE).astype(BF16)
    wqr = jnp.pad(wq3[..., MLA_NOPE:], ((0, 0), (0, 0), (0, 0), (0, LANES - MLA_ROPE))
                  ).reshape(depth, MLA_Q_RANK, MLA_HEADS * LANES).astype(BF16)
    wkv3 = w_ukv.reshape(depth, MLA_KV_RANK, MLA_HEADS, MLA_NOPE + MLA_V)
    wuk = wkv3[..., :MLA_NOPE].reshape(depth, MLA_KV_RANK, MLA_HEADS * MLA_NOPE).astype(BF16)
    wuv = wkv3[..., MLA_NOPE:].reshape(depth, MLA_KV_RANK, MLA_HEADS * MLA_V).astype(BF16)
    bf_rows = jnp.pad(b_f, ((0, 0), (0, LANES - FOX_HEADS))).reshape(depth, 1, LANES)
    return w_small, w_fox, w_gate, jnp.asarray(colscale), wqn, wqr, wuk, wuv, bf_rows


def kernel(x, p, positions, w_in, g_cq, g_ckv, w_uq, w_ukv, b_f, w_ya, w_yb, w_o, ln1_g, ln1_b,
           w_pq, sub_k1, sub_k2, u_tab, v_tab, w_pg, w_pe, ln2_g, ln2_b):
    bsz, seq, d = x.shape
    depth = w_in.shape[0]
    n = bsz * seq
    assert bsz == 1, "token-major layout assumes a single sequence"
    alpha = (2 * depth) ** 0.25

    h = x.reshape(n, d)
    h_b = h.astype(BF16)
    tabs = _rope_tables(positions.reshape(n, 1))

    w_small, w_fox, w_gate, colscale, wqn, wqr, wuk, wuv, bf_rows = _mixer_weights(w_in, w_uq, w_ukv, b_f)
    w_ya_b, w_yb_b, w_o_b = w_ya.astype(BF16), w_yb.astype(BF16), w_o.astype(BF16)
    w_pq_b, w_pg_b, w_pe_b = w_pq.astype(BF16), w_pg.astype(BF16), w_pe.astype(BF16)
    pow2_scale = lambda amax: jnp.exp2(jnp.floor(jnp.log2(F8_TARGET_MAX / jnp.maximum(amax, 1e-30))))
    u_scale = pow2_scale(jnp.max(jnp.abs(u_tab), axis=(1, 2)))
    h_scale = pow2_scale(math.sqrt(d) * jnp.max(jnp.abs(ln1_g), axis=1) + jnp.max(jnp.abs(ln1_b), axis=1))
    u_q = (u_tab * u_scale[:, None, None]).astype(F8)
    inv_scale = (1.0 / (u_scale * h_scale)).astype(F32)
    vt_b = v_tab.transpose(0, 2, 1).astype(BF16)
    p_b = p.reshape(depth, n, -1).astype(BF16)

    for i in range(depth):
        zs = _mm(h_b, w_small, F32, 1024, 256, name="in_proj_small", layer=i)
        zf = _mm(h_b, w_fox, BF16, 1024, 1024, colscale, name="in_proj_fox", layer=i)
        zg = _mm(h_b, w_gate, BF16, 1024, 1024, name="in_proj_gate", layer=i)
        q_mla = _mla_q(zs, g_cq[i].reshape(1, -1), wqn[i], wqr[i], tabs)
        k_mla, v_mla = _mla_kv(zs, g_ckv[i].reshape(1, -1), wuk[i], wuv[i], tabs)
        y_a = _attention(q_mla, k_mla, v_mla, MLA_HEADS, 2 * LANES, MLA_V, 0, 0, 0, "attn_mla")
        q_fox, k_fox = _fox_pack(zf, _fox_cum(zs, bf_rows[i]))
        y_b = _attention(q_fox, k_fox, zf, FOX_HEADS, 2 * LANES, FOX_DIM, 0, 0, 2 * FOX_HEADS,
                         "attn_fox")
        merged = _merge(y_a, y_b, w_ya_b, w_yb_b, zg, 0, d, i)
        mix = _mm(merged, w_o_b, BF16, 1024, 1024, name="out_proj", layer=i)
        h, h_b, ht_q = _ln(h, [mix], ln1_g[i].reshape(1, d), ln1_b[i].reshape(1, d), alpha,
                           h_scale[i].reshape(1, 1))

        qp = _mm(h_b, w_pq_b, F32, 1024, 1024, name="peer_query", layer=i)
        nofi, c, r2, e2 = _route(qp, sub_k1[i], sub_k2[i])
        ffn = _experts(u_q, vt_b, ht_q, inv_scale[i].reshape(1, 1),
                       nofi.transpose(1, 0, 2), c.transpose(1, 0, 2), r2, e2, i)
        ple = _ple(h_b, w_pg_b, p_b, w_pe_b, i)
        h, h_b = _ln(h, [ffn, ple], ln2_g[i].reshape(1, d), ln2_b[i].reshape(1, d), alpha)

    return h.reshape(bsz, seq, d)
```

```python
import functools
import math

import numpy as np
import jax
import jax.numpy as jnp
from jax import lax
from jax.experimental import pallas as pl
from jax.experimental.pallas import tpu as pltpu

F32 = jnp.float32
BF16 = jnp.bfloat16
F8 = jnp.float8_e4m3fn
F8_TARGET_MAX = 240.0

MLA_HEADS = 16
MLA_Q_RANK = 1024
MLA_KV_RANK = 512
MLA_NOPE = 128
MLA_ROPE = 64
MLA_V = 128
ROPE_THETA = 10000.0
FOX_HEADS = 16
FOX_DIM = 128
PEER_HEADS = 8
PEER_KEYS = 128
PEER_HALF = 128
PEER_TOPK = 16
LN_EPS = 1e-5
RMS_EPS = 1e-6
LANES = 128
BF16_SUBLANES = 16
LOG2E = math.log2(math.e)
NEG = -1e30
POS_SENTINEL = 1e9
MLA_HEAD_GROUP = 4
ATTN_CHUNK = 1024
ATTN_SUBBLOCKS = 2

VMEM_LIMIT = 56 * 1024 * 1024


def _cparams(n_axes):
    return pltpu.CompilerParams(dimension_semantics=("arbitrary",) * n_axes,
                                vmem_limit_bytes=VMEM_LIMIT)


def _mm_kernel(x_ref, w_ref, o_ref):
    o_ref[...] = jnp.dot(x_ref[...], w_ref[...], preferred_element_type=F32).astype(o_ref.dtype)


def _mm_scale_kernel(x_ref, w_ref, s_ref, o_ref):
    acc = jnp.dot(x_ref[...], w_ref[...], preferred_element_type=F32)
    o_ref[...] = (acc * s_ref[...]).astype(o_ref.dtype)


def _wspec(layer, block, index_map):
    if layer is None:
        return pl.BlockSpec(block, index_map)
    return pl.BlockSpec((None,) + block, lambda *g: (layer,) + index_map(*g))


def _mm(x, w, out_dtype, tm, tn, colscale=None, name="mm", layer=None):
    m, k = x.shape
    n = w.shape[-1]
    tm, tn = min(tm, m), min(tn, n)
    assert m % tm == 0 and n % tn == 0
    in_specs = [pl.BlockSpec((tm, k), lambda i, j: (i, 0)),
                _wspec(layer, (k, tn), lambda i, j: (0, j))]
    args = [x, w]
    body = _mm_kernel
    if colscale is not None:
        in_specs.append(pl.BlockSpec((1, tn), lambda i, j: (0, j)))
        args.append(colscale)
        body = _mm_scale_kernel
    return pl.pallas_call(
        body,
        grid=(m // tm, n // tn),
        in_specs=in_specs,
        out_specs=pl.BlockSpec((tm, tn), lambda i, j: (i, j)),
        out_shape=jax.ShapeDtypeStruct((m, n), out_dtype),
        compiler_params=_cparams(2),
        name=name,
    )(*args)


def _rope_table_kernel(pos_ref, c_ref, cos_ref, sina_ref, sinb_ref):
    ang = pos_ref[...].astype(F32) * c_ref[0:1, :]
    cos = jnp.cos(ang)
    sin = jnp.sin(ang)
    cos_ref[...] = cos * c_ref[1:2, :]
    sina_ref[...] = sin * c_ref[2:3, :]
    sinb_ref[...] = sin * c_ref[3:4, :]


def _rope_tables(positions_col):
    n = positions_col.shape[0]
    tm = min(1024, n)
    half = MLA_ROPE // 2
    inv_freq = (ROPE_THETA ** (-np.arange(half, dtype=np.float32) / half)).astype(np.float32)
    consts = np.zeros((8, LANES), np.float32)
    consts[0, :half] = inv_freq
    consts[0, half:2 * half] = inv_freq
    consts[1, :2 * half] = 1.0
    consts[2, :half] = -1.0
    consts[3, half:2 * half] = 1.0
    shp = jax.ShapeDtypeStruct((n, LANES), F32)
    return pl.pallas_call(
        _rope_table_kernel,
        grid=(n // tm,),
        in_specs=[pl.BlockSpec((tm, 1), lambda i: (i, 0)),
                  pl.BlockSpec((8, LANES), lambda i: (0, 0))],
        out_specs=[pl.BlockSpec((tm, LANES), lambda i: (i, 0))] * 3,
        out_shape=[shp, shp, shp],
        compiler_params=_cparams(1),
        name="rope_tables",
    )(positions_col, jnp.asarray(consts))


def _rope128(x, cos, sina, sinb):
    return (x * cos + pltpu.roll(x, 3 * (MLA_ROPE // 2), 1) * sina
            + pltpu.roll(x, MLA_ROPE // 2, 1) * sinb)


def _rms(x, g):
    return x * lax.rsqrt(jnp.mean(x * x, axis=-1, keepdims=True) + RMS_EPS) * g


def _mla_q_kernel(cq_ref, g_ref, wn_ref, wr_ref, cos_ref, sina_ref, sinb_ref, q_ref, *, scale):
    xn = _rms(cq_ref[...], g_ref[...]).astype(BF16)
    qn = jnp.dot(xn, wn_ref[...], preferred_element_type=F32) * scale
    qr = jnp.dot(xn, wr_ref[...], preferred_element_type=F32) * scale
    cos, sina, sinb = cos_ref[...], sina_ref[...], sinb_ref[...]
    for hh in range(wn_ref.shape[1] // LANES):
        lo = hh * LANES
        q_ref[:, 2 * lo:2 * lo + LANES] = qn[:, lo:lo + LANES].astype(q_ref.dtype)
        q_ref[:, 2 * lo + LANES:2 * lo + 2 * LANES] = _rope128(
            qr[:, lo:lo + LANES], cos, sina, sinb).astype(q_ref.dtype)


def _mla_q(zs, g_cq, wqn, wqr, tabs):
    n = zs.shape[0]
    tm = min(1024, n)
    grp = MLA_HEAD_GROUP
    scale = (MLA_NOPE + MLA_ROPE) ** -0.5 * LOG2E
    tab_spec = pl.BlockSpec((tm, LANES), lambda i, p: (i, 0))
    return pl.pallas_call(
        functools.partial(_mla_q_kernel, scale=scale),
        grid=(n // tm, MLA_HEADS // grp),
        in_specs=[pl.BlockSpec((tm, MLA_Q_RANK), lambda i, p: (i, 0)),
                  pl.BlockSpec((1, MLA_Q_RANK), lambda i, p: (0, 0)),
                  pl.BlockSpec((MLA_Q_RANK, grp * LANES), lambda i, p: (0, p)),
                  pl.BlockSpec((MLA_Q_RANK, grp * LANES), lambda i, p: (0, p)),
                  tab_spec, tab_spec, tab_spec],
        out_specs=pl.BlockSpec((tm, grp * 2 * LANES), lambda i, p: (i, p)),
        out_shape=jax.ShapeDtypeStruct((n, MLA_HEADS * 2 * LANES), BF16),
        compiler_params=_cparams(2),
        name="mla_q",
    )(zs, g_cq, wqn, wqr, *tabs)


def _mla_kv_kernel(ckv_ref, g_ref, kr_ref, wk_ref, wv_ref, cos_ref, sina_ref, sinb_ref,
                   k_ref, v_ref):
    xn = _rms(ckv_ref[...], g_ref[...]).astype(BF16)
    kn = jnp.dot(xn, wk_ref[...], preferred_element_type=F32)
    v_ref[...] = jnp.dot(xn, wv_ref[...], preferred_element_type=F32).astype(v_ref.dtype)
    kr = _rope128(kr_ref[...], cos_ref[...], sina_ref[...], sinb_ref[...]).astype(k_ref.dtype)
    for hh in range(wk_ref.shape[1] // LANES):
        lo = hh * LANES
        k_ref[:, 2 * lo:2 * lo + LANES] = kn[:, lo:lo + LANES].astype(k_ref.dtype)
        k_ref[:, 2 * lo + LANES:2 * lo + 2 * LANES] = kr


def _mla_kv(zs, g_ckv, wuk, wuv, tabs):
    n = zs.shape[0]
    tm = min(1024, n)
    grp = MLA_HEAD_GROUP
    ckv_blk = MLA_Q_RANK // MLA_KV_RANK
    kr_blk = (MLA_Q_RANK + MLA_KV_RANK) // LANES
    tab_spec = pl.BlockSpec((tm, LANES), lambda i, p: (i, 0))
    return pl.pallas_call(
        _mla_kv_kernel,
        grid=(n // tm, MLA_HEADS // grp),
        in_specs=[pl.BlockSpec((tm, MLA_KV_RANK), lambda i, p: (i, ckv_blk)),
                  pl.BlockSpec((1, MLA_KV_RANK), lambda i, p: (0, 0)),
                  pl.BlockSpec((tm, LANES), lambda i, p: (i, kr_blk)),
                  pl.BlockSpec((MLA_KV_RANK, grp * LANES), lambda i, p: (0, p)),
                  pl.BlockSpec((MLA_KV_RANK, grp * LANES), lambda i, p: (0, p)),
                  tab_spec, tab_spec, tab_spec],
        out_specs=[pl.BlockSpec((tm, grp * 2 * LANES), lambda i, p: (i, p)),
                   pl.BlockSpec((tm, grp * LANES), lambda i, p: (i, p))],
        out_shape=[jax.ShapeDtypeStruct((n, MLA_HEADS * 2 * LANES), BF16),
                   jax.ShapeDtypeStruct((n, MLA_HEADS * MLA_V), BF16)],
        compiler_params=_cparams(2),
        name="mla_kv",
    )(zs, g_ckv, zs, wuk, wuv, *tabs)


def _split3(x):
    hi = x.astype(BF16)
    r = x - hi.astype(F32)
    mid = r.astype(BF16)
    lo = (r - mid.astype(F32)).astype(BF16)
    return hi, mid, lo


def _cum_kernel(fl_ref, bf_ref, tri_ref, o_ref, carry_ref):
    @pl.when(pl.program_id(0) == 0)
    def _():
        carry_ref[...] = jnp.zeros_like(carry_ref)

    x = fl_ref[...] + bf_ref[...]
    lf = jnp.minimum(x, 0.0) - jnp.log1p(jnp.exp(-jnp.abs(x)))
    tri = tri_ref[...]
    c = carry_ref[...]
    for part in _split3(lf):
        c = c + jnp.dot(tri, part, preferred_element_type=F32)
    carry_ref[...] = c[c.shape[0] - 1:, :]
    o_ref[...] = c


def _fox_cum(zs, bf_row):
    n = zs.shape[0]
    tm = 256
    fl_blk = (MLA_Q_RANK + MLA_KV_RANK) // LANES + 1
    tri = jnp.asarray(np.tril(np.ones((tm, tm), np.float32)), dtype=BF16)
    return pl.pallas_call(
        _cum_kernel,
        grid=(n // tm,),
        in_specs=[pl.BlockSpec((tm, LANES), lambda i: (i, fl_blk)),
                  pl.BlockSpec((1, LANES), lambda i: (0, 0)),
                  pl.BlockSpec((tm, tm), lambda i: (0, 0))],
        out_specs=pl.BlockSpec((tm, LANES), lambda i: (i, 0)),
        out_shape=jax.ShapeDtypeStruct((n, LANES), F32),
        scratch_shapes=[pltpu.VMEM((1, LANES), F32)],
        compiler_params=_cparams(1),
        name="fox_cum",
    )(zs, bf_row, tri)


def _fox_pack_kernel(fq_ref, fk_ref, cum_ref, e_ref, q_ref, k_ref):
    c3 = jnp.concatenate(_split3(cum_ref[...] * LOG2E), axis=1)
    kx = jnp.dot(c3, e_ref[0], preferred_element_type=F32)
    lane = lax.broadcasted_iota(jnp.int32, (fq_ref.shape[0], LANES), 1)
    qaug = jnp.where(lane < 3, -1.0, 0.0).astype(q_ref.dtype)
    for hh in range(2):
        lo = hh * LANES
        q_ref[:, 2 * lo:2 * lo + LANES] = fq_ref[:, lo:lo + LANES]
        q_ref[:, 2 * lo + LANES:2 * lo + 2 * LANES] = qaug
        k_ref[:, 2 * lo:2 * lo + LANES] = fk_ref[:, lo:lo + LANES]
        k_ref[:, 2 * lo + LANES:2 * lo + 2 * LANES] = kx[:, 2 * lo + LANES:2 * lo + 2 * LANES].astype(k_ref.dtype)


def _fox_pack(zb, cum):
    n = zb.shape[0]
    tm = min(1024, n)
    pairs = FOX_HEADS // 2
    place = np.zeros((pairs, 3 * LANES, 4 * LANES), np.float32)
    for p in range(pairs):
        for hh in range(2):
            for part in range(3):
                place[p, part * LANES + 2 * p + hh, hh * 2 * LANES + LANES + part] = 1.0
    out = jax.ShapeDtypeStruct((n, FOX_HEADS * 2 * LANES), BF16)
    ospec = pl.BlockSpec((tm, 4 * LANES), lambda i, p: (i, p))
    return pl.pallas_call(
        _fox_pack_kernel,
        grid=(n // tm, pairs),
        in_specs=[pl.BlockSpec((tm, 2 * LANES), lambda i, p: (i, p)),
                  pl.BlockSpec((tm, 2 * LANES), lambda i, p: (i, pairs + p)),
                  pl.BlockSpec((tm, LANES), lambda i, p: (i, 0)),
                  pl.BlockSpec((1, 3 * LANES, 4 * LANES), lambda i, p: (p, 0, 0))],
        out_specs=[ospec, ospec],
        out_shape=[out, out],
        compiler_params=_cparams(2),
        name="fox_pack",
    )(zb, zb, cum, jnp.asarray(place, dtype=BF16))


def _attn_kernel(q_ref, k_ref, v_ref, o_ref, vt_sc, s_sc, m_sc, acc_sc, *, tc, nsub, dv):
    i = pl.program_id(1)
    nchunks, dva, _ = vt_sc.shape

    @pl.when(i == 0)
    def _():
        ones_row = (lax.broadcasted_iota(jnp.int32, (dva - dv, tc), 0) == 0).astype(vt_sc.dtype)

        def fill(c, carry):
            off = pl.multiple_of(c * tc, tc)
            vt_sc[c, 0:dv, :] = v_ref[pl.ds(off, tc), :].astype(F32).T.astype(vt_sc.dtype)
            vt_sc[c, dv:dva, :] = ones_row
            return carry

        lax.fori_loop(0, nchunks, fill, 0)

    m_sc[...] = jnp.full_like(m_sc, NEG)
    acc_sc[...] = jnp.zeros_like(acc_sc)

    def scores(j, rs, slot):
        off = pl.multiple_of(j * tc, tc)
        k = k_ref[pl.ds(off, tc), :]
        for r in rs:
            s_sc[slot, r] = lax.dot_general(k, q_ref[r * tc:(r + 1) * tc, :],
                                            (((1,), (1,)), ((), ())), preferred_element_type=F32)

    def update(j, rs, slot, masked_r):
        for r in rs:
            s = s_sc[slot, r]
            if r == masked_r:
                key = lax.broadcasted_iota(jnp.int32, (tc, tc), 0)
                qry = lax.broadcasted_iota(jnp.int32, (tc, tc), 1)
                s = jnp.where(key <= qry, s, NEG)
            m_prev = m_sc[r]
            m_new = jnp.maximum(m_prev, jnp.max(s, axis=0, keepdims=True))
            alpha = jnp.exp2(m_prev - m_new)
            p = jnp.exp2(s - m_new).astype(vt_sc.dtype)
            acc_sc[r] = alpha * acc_sc[r] + jnp.dot(vt_sc[j], p, preferred_element_type=F32)
            m_sc[r] = m_new

    all_r = range(nsub)
    n_main = i * nsub
    scores(0, all_r, 0)

    def body(t, carry):
        j = 2 * t
        scores(j + 1, all_r, 1)
        update(j, all_r, 0, -1)
        scores(j + 2, all_r, 0)
        update(j + 1, all_r, 1, -1)
        return carry

    lax.fori_loop(0, n_main // 2, body, 0)
    for jj in range(nsub):
        if jj + 1 < nsub:
            scores(n_main + jj + 1, range(jj + 1, nsub), (jj + 1) % 2)
        update(n_main + jj, range(jj, nsub), jj % 2, jj)
    for r in range(nsub):
        acc = acc_sc[r]
        o = acc[0:dv, :] / acc[dv:dv + 1, :]
        o_ref[r * tc:(r + 1) * tc, :] = o.T.astype(o_ref.dtype)


def _attention(q_arr, k_arr, v_arr, heads, dk, dv, q_blk0, k_blk0, v_blk0, name):
    n = q_arr.shape[0]
    tc = min(ATTN_CHUNK, n)
    nsub = min(ATTN_SUBBLOCKS, n // tc)
    tq = tc * nsub
    assert nsub % 2 == 0 and n % tq == 0
    dva = dv + BF16_SUBLANES
    return pl.pallas_call(
        functools.partial(_attn_kernel, tc=tc, nsub=nsub, dv=dv),
        grid=(heads, n // tq),
        in_specs=[pl.BlockSpec((tq, dk), lambda h, i: (i, q_blk0 + h)),
                  pl.BlockSpec((n, dk), lambda h, i: (0, k_blk0 + h)),
                  pl.BlockSpec((n, dv), lambda h, i: (0, v_blk0 + h))],
        out_specs=pl.BlockSpec((tq, dv), lambda h, i: (i, h)),
        out_shape=jax.ShapeDtypeStruct((n, heads * dv), BF16),
        scratch_shapes=[pltpu.VMEM((n // tc, dva, tc), BF16),
                        pltpu.VMEM((2, nsub, tc, tc), F32),
                        pltpu.VMEM((nsub, 1, tc), F32),
                        pltpu.VMEM((nsub, dva, tc), F32)],
        compiler_params=_cparams(2),
        name=name,
    )(q_arr, k_arr, v_arr)


def _merge_kernel(ya_ref, yb_ref, wa_ref, wb_ref, ga_ref, gb_ref, o_ref):
    a = jnp.dot(ya_ref[...], wa_ref[...], preferred_element_type=F32)
    b = jnp.dot(yb_ref[...], wb_ref[...], preferred_element_type=F32)
    ga = jax.nn.sigmoid(ga_ref[...].astype(F32))
    gb = jax.nn.sigmoid(gb_ref[...].astype(F32))
    o_ref[...] = (ga * a + gb * b).astype(o_ref.dtype)


def _merge(ya, yb, w_ya, w_yb, zb, ga_col0, gb_col0, layer):
    n, ka = ya.shape
    kb = yb.shape[1]
    d = w_ya.shape[-1]
    tm, tn = min(1024, n), 512
    ga0, gb0 = ga_col0 // tn, gb_col0 // tn
    return pl.pallas_call(
        _merge_kernel,
        grid=(n // tm, d // tn),
        in_specs=[pl.BlockSpec((tm, ka), lambda i, j: (i, 0)),
                  pl.BlockSpec((tm, kb), lambda i, j: (i, 0)),
                  _wspec(layer, (ka, tn), lambda i, j: (0, j)),
                  _wspec(layer, (kb, tn), lambda i, j: (0, j)),
                  pl.BlockSpec((tm, tn), lambda i, j: (i, ga0 + j)),
                  pl.BlockSpec((tm, tn), lambda i, j: (i, gb0 + j))],
        out_specs=pl.BlockSpec((tm, tn), lambda i, j: (i, j)),
        out_shape=jax.ShapeDtypeStruct((n, d), BF16),
        compiler_params=_cparams(2),
        name="merge",
    )(ya, yb, w_ya, w_yb, zb, zb)


def _ln_kernel(*refs, alpha, n_add, with_row, with_t):
    h_ref = refs[0]
    add_refs = refs[1:1 + n_add]
    g_ref, b_ref = refs[1 + n_add:3 + n_add]
    quant = with_row or with_t
    outs = refs[3 + n_add + (1 if quant else 0):]
    x = alpha * h_ref[...]
    for r in add_refs:
        x = x + r[...].astype(F32)
    mu = jnp.mean(x, axis=-1, keepdims=True)
    xc = x - mu
    var = jnp.mean(xc * xc, axis=-1, keepdims=True)
    y = xc * lax.rsqrt(var + LN_EPS) * g_ref[...] + b_ref[...]
    outs[0][...] = y
    outs[1][...] = y.astype(outs[1].dtype)
    if quant:
        ys = y * refs[3 + n_add][...]
        k = 2
        if with_row:
            outs[k][...] = ys.astype(outs[k].dtype)
            k += 1
        if with_t:
            outs[k][...] = ys.T.astype(outs[k].dtype)


def _ln(h, addends, g, b, alpha, q_scale=None, with_row=False, with_t=False):
    n, d = h.shape
    tm = min(256, n)
    assert (q_scale is not None) == (with_row or with_t)
    row = pl.BlockSpec((tm, d), lambda i: (i, 0))
    vec = pl.BlockSpec((1, d), lambda i: (0, 0))
    in_specs = [row] * (1 + len(addends)) + [vec, vec]
    args = [h, *addends, g, b]
    out_specs = [row, row]
    out_shape = [jax.ShapeDtypeStruct((n, d), F32), jax.ShapeDtypeStruct((n, d), BF16)]
    if q_scale is not None:
        in_specs.append(pl.BlockSpec((1, 1), lambda i: (0, 0)))
        args.append(q_scale)
    if with_row:
        out_specs.append(row)
        out_shape.append(jax.ShapeDtypeStruct((n, d), F8))
    if with_t:
        out_specs.append(pl.BlockSpec((d, tm), lambda i: (0, i)))
        out_shape.append(jax.ShapeDtypeStruct((d, n), F8))
    return pl.pallas_call(
        functools.partial(_ln_kernel, alpha=alpha, n_add=len(addends), with_row=with_row, with_t=with_t),
        grid=(n // tm,),
        in_specs=in_specs,
        out_specs=out_specs,
        out_shape=out_shape,
        compiler_params=_cparams(1),
        name="deepnorm_ln",
    )(*args)


def _topk_rows(problems, k, pos=None):
    r, t = problems[0][0].shape
    iota = lax.broadcasted_iota(jnp.int32, (r, t), 0).astype(F32) if pos is None else pos
    for _, _, _, rank_ref in problems:
        if rank_ref is not None:
            rank_ref[...] = jnp.full((r, t), float(k), F32)

    def body(kk, carry):
        for x_ref, val_ref, idx_ref, rank_ref in problems:
            x = x_ref[...]
            m = jnp.max(x, axis=0, keepdims=True)
            idx = jnp.min(jnp.where(x == m, iota, POS_SENTINEL), axis=0, keepdims=True)
            sel = iota == idx
            x_ref[...] = jnp.where(sel, -jnp.inf, x)
            if rank_ref is not None:
                rank_ref[...] = jnp.where(sel, lax.convert_element_type(kk, F32), rank_ref[...])
            val_ref[pl.ds(kk, 1), :] = m
            idx_ref[pl.ds(kk, 1), :] = idx
        return carry

    lax.fori_loop(0, k, body, 0)


def _candidate_groups(k):
    assert k == 16
    return [(0, 16), (1, 8), (2, 8), (3, 8), (4, 4), (5, 4), (6, 4), (7, 4)] + [(a, 1) for a in range(8, 16)]


def _route_kernel(qp_ref, k1_ref, k2_ref, cpos_ref, nofi_ref, c_ref, r2_ref, e2_ref,
                  s_sc, cand_sc, t1_sc, i1_sc, t2_sc, i2_sc, top_sc, pos_sc, r1_sc, r2_sc):
    kk = PEER_TOPK
    qp = qp_ref[...].astype(BF16)
    nt = (((1,), (1,)), ((), ()))
    s1 = lax.dot_general(k1_ref[0].astype(BF16), qp[:, :PEER_HALF], nt, preferred_element_type=F32)
    s2 = lax.dot_general(k2_ref[0].astype(BF16), qp[:, PEER_HALF:], nt, preferred_element_type=F32)

    s_sc[0] = s1
    s_sc[1] = s2
    _topk_rows([(s_sc.at[0], t1_sc, i1_sc, r1_sc), (s_sc.at[1], t2_sc, i2_sc, r2_sc)], kk)

    row = 0
    for a, nb in _candidate_groups(kk):
        cand_sc[row:row + nb, :] = t1_sc[a:a + 1, :] + t2_sc[0:nb, :]
        row += nb
    _topk_rows([(cand_sc, top_sc, pos_sc, None)], kk, pos=cpos_ref[...])

    top = top_sc[...]
    z = jnp.sum(jnp.exp(top - top[0:1, :]), axis=0, keepdims=True)
    a_sel = jnp.floor(pos_sc[...] * (1.0 / kk))
    r1 = r1_sc[...]
    nofi = jnp.zeros_like(r1)
    for a in range(kk):
        n_a = jnp.sum((a_sel == float(a)).astype(F32), axis=0, keepdims=True)
        nofi = nofi + jnp.where(r1 == float(a), n_a, 0.0)
    nofi_ref[0] = nofi
    c_ref[0] = jnp.exp(s1 - t1_sc[0:1, :]) / z
    r2_ref[0] = r2_sc[...].astype(r2_ref.dtype)
    e2_ref[0] = jnp.exp(s2 - t2_sc[0:1, :]).astype(e2_ref.dtype)


def _route(qp, sub_k1, sub_k2):
    n = qp.shape[0]
    tr = min(512, n)
    kk = PEER_TOPK
    out = jax.ShapeDtypeStruct((PEER_HEADS, PEER_KEYS, n), F32)
    out_b = jax.ShapeDtypeStruct((PEER_HEADS, PEER_KEYS, n), BF16)
    ospec = pl.BlockSpec((1, PEER_KEYS, tr), lambda i, h: (h, 0, i))
    kspec = pl.BlockSpec((1, PEER_KEYS, PEER_HALF), lambda i, h: (h, 0, 0))
    small = pltpu.VMEM((kk, tr), F32)
    cpos = np.concatenate([a * kk + np.arange(nb) for a, nb in _candidate_groups(kk)]).astype(np.float32)
    ncand = cpos.shape[0]
    cpos = jnp.asarray(np.broadcast_to(cpos[:, None], (ncand, tr)))
    return pl.pallas_call(
        _route_kernel,
        grid=(n // tr, PEER_HEADS),
        in_specs=[pl.BlockSpec((tr, 2 * PEER_HALF), lambda i, h: (i, h)), kspec, kspec,
                  pl.BlockSpec((ncand, tr), lambda i, h: (0, 0))],
        out_specs=[ospec] * 4,
        out_shape=[out, out, out_b, out_b],
        scratch_shapes=[pltpu.VMEM((2, PEER_KEYS, tr), F32), pltpu.VMEM((ncand, tr), F32),
                        small, small, small, small, small, small,
                        pltpu.VMEM((PEER_KEYS, tr), F32), pltpu.VMEM((PEER_KEYS, tr), F32)],
        compiler_params=_cparams(2),
        name="peer_route",
    )(qp, sub_k1, sub_k2, cpos)


def _gelu_tanh(x):
    c = math.sqrt(2.0 / math.pi)
    return 0.5 * x * (1.0 + jnp.tanh(c * (x + 0.044715 * (x * x * x))))


def _gates(nofi_ref, c_ref, r2_ref, e2_ref, w_sc):
    for il in range(nofi_ref.shape[0]):
        w = None
        for h in range(PEER_HEADS):
            n_i = nofi_ref[il, h:h + 1, :].astype(BF16)
            c_i = c_ref[il, h:h + 1, :].astype(BF16)
            term = jnp.where(r2_ref[h] < n_i, e2_ref[h] * c_i, jnp.zeros((), BF16))
            w = term if w is None else w + term
        w_sc[il * PEER_KEYS:(il + 1) * PEER_KEYS, :] = w


def _expert_kernel(u_ref, vt_ref, ht_ref, inv_ref, nofi0_ref, c0_ref, nofi_ref, c_ref, r2_ref, e2_ref,
                   o_ref, acc_sc, g_sc, w_sc):
    e = pl.program_id(1)
    old = lax.rem(e + 1, 2)
    new = lax.rem(e, 2)

    @pl.when(e == 0)
    def _():
        acc_sc[...] = jnp.zeros_like(acc_sc)
        g_sc[...] = jnp.zeros_like(g_sc)
        _gates(nofi0_ref, c0_ref, r2_ref, e2_ref, w_sc.at[0])

    act = jnp.dot(u_ref[...], ht_ref[...], preferred_element_type=F32) * inv_ref[...]
    acc_sc[...] += jnp.dot(vt_ref[...], g_sc[old], preferred_element_type=F32)
    g_sc[new] = _gelu_tanh(act.astype(g_sc.dtype)) * w_sc[new]
    _gates(nofi_ref, c_ref, r2_ref, e2_ref, w_sc.at[old])

    @pl.when(e == pl.num_programs(1) - 1)
    def _():
        o_ref[...] = acc_sc[...].T.astype(o_ref.dtype)


def _experts(u_q, vt_b, ht_q, inv_scale, nofi_t, c_t, r2, e2, layer=None):
    d, n = ht_q.shape
    ne = u_q.shape[-2]
    tm = min(512, n)
    te = 512
    gi = te // PEER_KEYS
    nblk = ne // te
    cur = lambda e: jnp.minimum(e, nblk - 1)
    nxt = lambda e: jnp.minimum(e + 1, nblk - 1)
    prev = lambda e: jnp.maximum(e - 1, 0)
    ispec0 = pl.BlockSpec((gi, PEER_HEADS, tm), lambda b, e: (0, 0, b))
    ispec = pl.BlockSpec((gi, PEER_HEADS, tm), lambda b, e: (nxt(e), 0, b))
    jspec = pl.BlockSpec((PEER_HEADS, PEER_KEYS, tm), lambda b, e: (0, 0, b))
    return pl.pallas_call(
        _expert_kernel,
        grid=(n // tm, nblk + 1),
        in_specs=[_wspec(layer, (te, d), lambda b, e: (cur(e), 0)),
                  _wspec(layer, (d, te), lambda b, e: (0, prev(e))),
                  pl.BlockSpec((d, tm), lambda b, e: (0, b)),
                  pl.BlockSpec((1, 1), lambda b, e: (0, 0)),
                  ispec0, ispec0, ispec, ispec, jspec, jspec],
        out_specs=pl.BlockSpec((tm, d), lambda b, e: (b, 0)),
        out_shape=jax.ShapeDtypeStruct((n, d), BF16),
        scratch_shapes=[pltpu.VMEM((d, tm), F32), pltpu.VMEM((2, te, tm), BF16),
                        pltpu.VMEM((2, te, tm), BF16)],
        compiler_params=_cparams(2),
        name="peer_experts",
    )(u_q, vt_b, ht_q, inv_scale, nofi_t, c_t, nofi_t, c_t, r2, e2)


def _ple_kernel(h_ref, wg_ref, inv_ref, p_ref, wp_ref, o_ref):
    g = jnp.dot(h_ref[...], wg_ref[...], preferred_element_type=F32) * inv_ref[...]
    pp = jnp.dot(p_ref[...], wp_ref[...], preferred_element_type=F32)
    o_ref[...] = (jax.nn.sigmoid(g) * pp).astype(o_ref.dtype)


def _ple(h_q, w_pg_q, inv_scale, p_b, w_pe, layer):
    n, d = h_q.shape
    kp = p_b.shape[-1]
    tm, tn = min(1024, n), 1024
    return pl.pallas_call(
        _ple_kernel,
        grid=(n // tm, d // tn),
        in_specs=[pl.BlockSpec((tm, d), lambda i, j: (i, 0)),
                  _wspec(layer, (d, tn), lambda i, j: (0, j)),
                  pl.BlockSpec((1, 1), lambda i, j: (0, 0)),
                  _wspec(layer, (tm, kp), lambda i, j: (i, 0)),
                  _wspec(layer, (kp, tn), lambda i, j: (0, j))],
        out_specs=pl.BlockSpec((tm, tn), lambda i, j: (i, j)),
        out_shape=jax.ShapeDtypeStruct((n, d), BF16),
        compiler_params=_cparams(2),
        name="ple",
    )(h_q, w_pg_q, inv_scale, p_b, w_pe)


def _pow2_scale(amax):
    return jnp.exp2(jnp.floor(jnp.log2(F8_TARGET_MAX / jnp.maximum(amax, 1e-30))))


def _ln_out_scale(g, b, d):
    return _pow2_scale(math.sqrt(d) * jnp.max(jnp.abs(g), axis=1) + jnp.max(jnp.abs(b), axis=1))


def _mixer_weights(w_in, w_uq, w_ukv, b_f):
    depth, d, _ = w_in.shape
    sizes = (MLA_Q_RANK, MLA_KV_RANK, MLA_ROPE, FOX_HEADS * FOX_DIM, FOX_HEADS * FOX_DIM,
             FOX_HEADS * FOX_DIM, FOX_HEADS, d, d)
    offs = np.concatenate([[0], np.cumsum(sizes)])
    cq, ckv, kr, _, _, _, fl, _, _ = [w_in[:, :, offs[t]:offs[t + 1]] for t in range(9)]
    zpad = lambda c: jnp.zeros((depth, d, c), w_in.dtype)
    w_small = jnp.concatenate([cq, ckv, kr, zpad(LANES - MLA_ROPE), fl, zpad(LANES - FOX_HEADS)],
                              axis=2).astype(BF16)
    w_fox = w_in[:, :, offs[3]:offs[6]].astype(BF16)
    w_gate = w_in[:, :, offs[7]:offs[9]]
    gate_scale = _pow2_scale(jnp.max(jnp.abs(w_gate), axis=(1, 2)))
    w_gate_q = (w_gate * gate_scale[:, None, None]).astype(F8)
    colscale = np.ones((1, w_fox.shape[2]), np.float32)
    colscale[0, :FOX_HEADS * FOX_DIM] = FOX_DIM ** -0.5 * LOG2E
    qk = MLA_NOPE + MLA_ROPE
    wq3 = w_uq.reshape(depth, MLA_Q_RANK, MLA_HEADS, qk)
    wqn = wq3[..., :MLA_NOPE].reshape(depth, MLA_Q_RANK, MLA_HEADS * MLA_NOPE).astype(BF16)
    wqr = jnp.pad(wq3[..., MLA_NOPE:], ((0, 0), (0, 0), (0, 0), (0, LANES - MLA_ROPE))
                  ).reshape(depth, MLA_Q_RANK, MLA_HEADS * LANES).astype(BF16)
    wkv3 = w_ukv.reshape(depth, MLA_KV_RANK, MLA_HEADS, MLA_NOPE + MLA_V)
    wuk = wkv3[..., :MLA_NOPE].reshape(depth, MLA_KV_RANK, MLA_HEADS * MLA_NOPE).astype(BF16)
    wuv = wkv3[..., MLA_NOPE:].reshape(depth, MLA_KV_RANK, MLA_HEADS * MLA_V).astype(BF16)
    bf_rows = jnp.pad(b_f, ((0, 0), (0, LANES - FOX_HEADS))).reshape(depth, 1, LANES)
    return w_small, w_fox, w_gate_q, gate_scale, jnp.asarray(colscale), wqn, wqr, wuk, wuv, bf_rows


def kernel(x, p, positions, w_in, g_cq, g_ckv, w_uq, w_ukv, b_f, w_ya, w_yb, w_o, ln1_g, ln1_b,
           w_pq, sub_k1, sub_k2, u_tab, v_tab, w_pg, w_pe, ln2_g, ln2_b):
    bsz, seq, d = x.shape
    depth = w_in.shape[0]
    n = bsz * seq
    assert bsz == 1, "token-major layout assumes a single sequence"
    alpha = (2 * depth) ** 0.25

    h = x.reshape(n, d)
    h_b = h.astype(BF16)
    tabs = _rope_tables(positions.reshape(n, 1))

    (w_small, w_fox, w_gate_q, gate_scale, colscale, wqn, wqr, wuk, wuv, bf_rows) = _mixer_weights(
        w_in, w_uq, w_ukv, b_f)
    w_ya_b, w_yb_b, w_o_b = w_ya.astype(BF16), w_yb.astype(BF16), w_o.astype(BF16)
    w_pq_b, w_pe_b = w_pq.astype(BF16), w_pe.astype(BF16)
    vt_b = v_tab.transpose(0, 2, 1).astype(BF16)
    p_b = p.reshape(depth, n, -1).astype(BF16)
    u_scale = _pow2_scale(jnp.max(jnp.abs(u_tab), axis=(1, 2)))
    u_q = (u_tab * u_scale[:, None, None]).astype(F8)
    pg_scale = _pow2_scale(jnp.max(jnp.abs(w_pg), axis=(1, 2)))
    w_pg_q = (w_pg * pg_scale[:, None, None]).astype(F8)
    h1_scale = _ln_out_scale(ln1_g, ln1_b, d)
    h2_scale = _ln_out_scale(ln2_g, ln2_b, d)
    in_scale = _pow2_scale(jnp.max(jnp.abs(h)))
    h_q = (h * in_scale).astype(F8)

    for i in range(depth):
        zs = _mm(h_b, w_small, F32, 1024, 256, name="in_proj_small", layer=i)
        zf = _mm(h_b, w_fox, BF16, 1024, 1024, colscale, name="in_proj_fox", layer=i)
        gate_inv = jnp.full((1, 2 * d), 1.0, F32) / (in_scale * gate_scale[i])
        zg = _mm(h_q, w_gate_q, BF16, 1024, 1024, gate_inv, name="in_proj_gate", layer=i)
        q_mla = _mla_q(zs, g_cq[i].reshape(1, -1), wqn[i], wqr[i], tabs)
        k_mla, v_mla = _mla_kv(zs, g_ckv[i].reshape(1, -1), wuk[i], wuv[i], tabs)
        y_a = _attention(q_mla, k_mla, v_mla, MLA_HEADS, 2 * LANES, MLA_V, 0, 0, 0, "attn_mla")
        q_fox, k_fox = _fox_pack(zf, _fox_cum(zs, bf_rows[i]))
        y_b = _attention(q_fox, k_fox, zf, FOX_HEADS, 2 * LANES, FOX_DIM, 0, 0, 2 * FOX_HEADS,
                         "attn_fox")
        merged = _merge(y_a, y_b, w_ya_b, w_yb_b, zg, 0, d, i)
        mix = _mm(merged, w_o_b, BF16, 1024, 1024, name="out_proj", layer=i)
        h, h_b, h1_q, ht_q = _ln(h, [mix], ln1_g[i].reshape(1, d), ln1_b[i].reshape(1, d), alpha,
                                 h1_scale[i].reshape(1, 1), with_row=True, with_t=True)

        qp = _mm(h_b, w_pq_b, F32, 1024, 1024, name="peer_query", layer=i)
        nofi, c, r2, e2 = _route(qp, sub_k1[i], sub_k2[i])
        ffn = _experts(u_q, vt_b, ht_q, (1.0 / (u_scale[i] * h1_scale[i])).reshape(1, 1),
                       nofi.transpose(1, 0, 2), c.transpose(1, 0, 2), r2, e2, i)
        ple = _ple(h1_q, w_pg_q, (1.0 / (h1_scale[i] * pg_scale[i])).reshape(1, 1), p_b, w_pe_b, i)
        if i + 1 < depth:
            in_scale = h2_scale[i]
            h, h_b, h_q = _ln(h, [ffn, ple], ln2_g[i].reshape(1, d), ln2_b[i].reshape(1, d), alpha,
                              in_scale.reshape(1, 1), with_row=True)
        else:
            h, h_b = _ln(h, [ffn, ple], ln2_g[i].reshape(1, d), ln2_b[i].reshape(1, d), alpha)

    return h.reshape(bsz, seq, d)
```

```python
import functools
import math

import numpy as np
import jax
import jax.numpy as jnp
from jax import lax
from jax.experimental import pallas as pl
from jax.experimental.pallas import tpu as pltpu

F32 = jnp.float32
BF16 = jnp.bfloat16
F8 = jnp.float8_e4m3fn
F8_TARGET_MAX = 240.0

MLA_HEADS = 16
MLA_Q_RANK = 1024
MLA_KV_RANK = 512
MLA_NOPE = 128
MLA_ROPE = 64
MLA_V = 128
ROPE_THETA = 10000.0
FOX_HEADS = 16
FOX_DIM = 128
PEER_HEADS = 8
PEER_KEYS = 128
PEER_HALF = 128
PEER_TOPK = 16
LN_EPS = 1e-5
RMS_EPS = 1e-6
LANES = 128
BF16_SUBLANES = 16
LOG2E = math.log2(math.e)
NEG = -1e30
POS_SENTINEL = 1e9
MLA_HEAD_GROUP = 4
ATTN_CHUNK = 1024
ATTN_SUBBLOCKS = 2

VMEM_LIMIT = 56 * 1024 * 1024


def _cparams(n_axes):
    return pltpu.CompilerParams(dimension_semantics=("arbitrary",) * n_axes,
                                vmem_limit_bytes=VMEM_LIMIT)


def _mm_kernel(x_ref, w_ref, o_ref):
    o_ref[...] = jnp.dot(x_ref[...], w_ref[...], preferred_element_type=F32).astype(o_ref.dtype)


def _mm_scale_kernel(x_ref, w_ref, s_ref, o_ref):
    acc = jnp.dot(x_ref[...], w_ref[...], preferred_element_type=F32)
    o_ref[...] = (acc * s_ref[...]).astype(o_ref.dtype)


def _wspec(layer, block, index_map):
    if layer is None:
        return pl.BlockSpec(block, index_map)
    return pl.BlockSpec((None,) + block, lambda *g: (layer,) + index_map(*g))


def _mm(x, w, out_dtype, tm, tn, colscale=None, name="mm", layer=None):
    m, k = x.shape
    n = w.shape[-1]
    tm, tn = min(tm, m), min(tn, n)
    assert m % tm == 0 and n % tn == 0
    in_specs = [pl.BlockSpec((tm, k), lambda i, j: (i, 0)),
                _wspec(layer, (k, tn), lambda i, j: (0, j))]
    args = [x, w]
    body = _mm_kernel
    if colscale is not None:
        in_specs.append(pl.BlockSpec((1, tn), lambda i, j: (0, j)))
        args.append(colscale)
        body = _mm_scale_kernel
    return pl.pallas_call(
        body,
        grid=(m // tm, n // tn),
        in_specs=in_specs,
        out_specs=pl.BlockSpec((tm, tn), lambda i, j: (i, j)),
        out_shape=jax.ShapeDtypeStruct((m, n), out_dtype),
        compiler_params=_cparams(2),
        name=name,
    )(*args)


def _rope_table_kernel(pos_ref, c_ref, cos_ref, sina_ref, sinb_ref):
    ang = pos_ref[...].astype(F32) * c_ref[0:1, :]
    cos = jnp.cos(ang)
    sin = jnp.sin(ang)
    cos_ref[...] = cos * c_ref[1:2, :]
    sina_ref[...] = sin * c_ref[2:3, :]
    sinb_ref[...] = sin * c_ref[3:4, :]


def _rope_tables(positions_col):
    n = positions_col.shape[0]
    tm = min(1024, n)
    half = MLA_ROPE // 2
    inv_freq = (ROPE_THETA ** (-np.arange(half, dtype=np.float32) / half)).astype(np.float32)
    consts = np.zeros((8, LANES), np.float32)
    consts[0, :half] = inv_freq
    consts[0, half:2 * half] = inv_freq
    consts[1, :2 * half] = 1.0
    consts[2, :half] = -1.0
    consts[3, half:2 * half] = 1.0
    shp = jax.ShapeDtypeStruct((n, LANES), F32)
    return pl.pallas_call(
        _rope_table_kernel,
        grid=(n // tm,),
        in_specs=[pl.BlockSpec((tm, 1), lambda i: (i, 0)),
                  pl.BlockSpec((8, LANES), lambda i: (0, 0))],
        out_specs=[pl.BlockSpec((tm, LANES), lambda i: (i, 0))] * 3,
        out_shape=[shp, shp, shp],
        compiler_params=_cparams(1),
        name="rope_tables",
    )(positions_col, jnp.asarray(consts))


def _rope128(x, cos, sina, sinb):
    return (x * cos + pltpu.roll(x, 3 * (MLA_ROPE // 2), 1) * sina
            + pltpu.roll(x, MLA_ROPE // 2, 1) * sinb)


def _rms(x, g):
    return x * lax.rsqrt(jnp.mean(x * x, axis=-1, keepdims=True) + RMS_EPS) * g


def _mla_q_kernel(cq_ref, g_ref, wn_ref, wr_ref, cos_ref, sina_ref, sinb_ref, q_ref, *, scale):
    xn = _rms(cq_ref[...], g_ref[...]).astype(BF16)
    qn = jnp.dot(xn, wn_ref[...], preferred_element_type=F32) * scale
    qr = jnp.dot(xn, wr_ref[...], preferred_element_type=F32) * scale
    cos, sina, sinb = cos_ref[...], sina_ref[...], sinb_ref[...]
    for hh in range(wn_ref.shape[1] // LANES):
        lo = hh * LANES
        q_ref[:, 2 * lo:2 * lo + LANES] = qn[:, lo:lo + LANES].astype(q_ref.dtype)
        q_ref[:, 2 * lo + LANES:2 * lo + 2 * LANES] = _rope128(
            qr[:, lo:lo + LANES], cos, sina, sinb).astype(q_ref.dtype)


def _mla_q(zs, g_cq, wqn, wqr, tabs):
    n = zs.shape[0]
    tm = min(1024, n)
    grp = MLA_HEAD_GROUP
    scale = (MLA_NOPE + MLA_ROPE) ** -0.5 * LOG2E
    tab_spec = pl.BlockSpec((tm, LANES), lambda i, p: (i, 0))
    return pl.pallas_call(
        functools.partial(_mla_q_kernel, scale=scale),
        grid=(n // tm, MLA_HEADS // grp),
        in_specs=[pl.BlockSpec((tm, MLA_Q_RANK), lambda i, p: (i, 0)),
                  pl.BlockSpec((1, MLA_Q_RANK), lambda i, p: (0, 0)),
                  pl.BlockSpec((MLA_Q_RANK, grp * LANES), lambda i, p: (0, p)),
                  pl.BlockSpec((MLA_Q_RANK, grp * LANES), lambda i, p: (0, p)),
                  tab_spec, tab_spec, tab_spec],
        out_specs=pl.BlockSpec((tm, grp * 2 * LANES), lambda i, p: (i, p)),
        out_shape=jax.ShapeDtypeStruct((n, MLA_HEADS * 2 * LANES), BF16),
        compiler_params=_cparams(2),
        name="mla_q",
    )(zs, g_cq, wqn, wqr, *tabs)


def _mla_kv_kernel(ckv_ref, g_ref, kr_ref, wk_ref, wv_ref, cos_ref, sina_ref, sinb_ref,
                   k_ref, v_ref):
    xn = _rms(ckv_ref[...], g_ref[...]).astype(BF16)
    kn = jnp.dot(xn, wk_ref[...], preferred_element_type=F32)
    v_ref[...] = jnp.dot(xn, wv_ref[...], preferred_element_type=F32).astype(v_ref.dtype)
    kr = _rope128(kr_ref[...], cos_ref[...], sina_ref[...], sinb_ref[...]).astype(k_ref.dtype)
    for hh in range(wk_ref.shape[1] // LANES):
        lo = hh * LANES
        k_ref[:, 2 * lo:2 * lo + LANES] = kn[:, lo:lo + LANES].astype(k_ref.dtype)
        k_ref[:, 2 * lo + LANES:2 * lo + 2 * LANES] = kr


def _mla_kv(zs, g_ckv, wuk, wuv, tabs):
    n = zs.shape[0]
    tm = min(1024, n)
    grp = MLA_HEAD_GROUP
    ckv_blk = MLA_Q_RANK // MLA_KV_RANK
    kr_blk = (MLA_Q_RANK + MLA_KV_RANK) // LANES
    tab_spec = pl.BlockSpec((tm, LANES), lambda i, p: (i, 0))
    return pl.pallas_call(
        _mla_kv_kernel,
        grid=(n // tm, MLA_HEADS // grp),
        in_specs=[pl.BlockSpec((tm, MLA_KV_RANK), lambda i, p: (i, ckv_blk)),
                  pl.BlockSpec((1, MLA_KV_RANK), lambda i, p: (0, 0)),
                  pl.BlockSpec((tm, LANES), lambda i, p: (i, kr_blk)),
                  pl.BlockSpec((MLA_KV_RANK, grp * LANES), lambda i, p: (0, p)),
                  pl.BlockSpec((MLA_KV_RANK, grp * LANES), lambda i, p: (0, p)),
                  tab_spec, tab_spec, tab_spec],
        out_specs=[pl.BlockSpec((tm, grp * 2 * LANES), lambda i, p: (i, p)),
                   pl.BlockSpec((tm, grp * LANES), lambda i, p: (i, p))],
        out_shape=[jax.ShapeDtypeStruct((n, MLA_HEADS * 2 * LANES), BF16),
                   jax.ShapeDtypeStruct((n, MLA_HEADS * MLA_V), BF16)],
        compiler_params=_cparams(2),
        name="mla_kv",
    )(zs, g_ckv, zs, wuk, wuv, *tabs)


def _split3(x):
    hi = x.astype(BF16)
    r = x - hi.astype(F32)
    mid = r.astype(BF16)
    lo = (r - mid.astype(F32)).astype(BF16)
    return hi, mid, lo


def _cum_kernel(fl_ref, bf_ref, tri_ref, o_ref, carry_ref):
    @pl.when(pl.program_id(0) == 0)
    def _():
        carry_ref[...] = jnp.zeros_like(carry_ref)

    x = fl_ref[...] + bf_ref[...]
    lf = jnp.minimum(x, 0.0) - jnp.log1p(jnp.exp(-jnp.abs(x)))
    tri = tri_ref[...]
    c = carry_ref[...]
    for part in _split3(lf):
        c = c + jnp.dot(tri, part, preferred_element_type=F32)
    carry_ref[...] = c[c.shape[0] - 1:, :]
    o_ref[...] = c


def _fox_cum(zs, bf_row):
    n = zs.shape[0]
    tm = 256
    fl_blk = (MLA_Q_RANK + MLA_KV_RANK) // LANES + 1
    tri = jnp.asarray(np.tril(np.ones((tm, tm), np.float32)), dtype=BF16)
    return pl.pallas_call(
        _cum_kernel,
        grid=(n // tm,),
        in_specs=[pl.BlockSpec((tm, LANES), lambda i: (i, fl_blk)),
                  pl.BlockSpec((1, LANES), lambda i: (0, 0)),
                  pl.BlockSpec((tm, tm), lambda i: (0, 0))],
        out_specs=pl.BlockSpec((tm, LANES), lambda i: (i, 0)),
        out_shape=jax.ShapeDtypeStruct((n, LANES), F32),
        scratch_shapes=[pltpu.VMEM((1, LANES), F32)],
        compiler_params=_cparams(1),
        name="fox_cum",
    )(zs, bf_row, tri)


def _fox_pack_kernel(fq_ref, fk_ref, cum_ref, e_ref, q_ref, k_ref):
    c3 = jnp.concatenate(_split3(cum_ref[...] * LOG2E), axis=1)
    kx = jnp.dot(c3, e_ref[0], preferred_element_type=F32)
    lane = lax.broadcasted_iota(jnp.int32, (fq_ref.shape[0], LANES), 1)
    qaug = jnp.where(lane < 3, -1.0, 0.0).astype(q_ref.dtype)
    for hh in range(2):
        lo = hh * LANES
        q_ref[:, 2 * lo:2 * lo + LANES] = fq_ref[:, lo:lo + LANES]
        q_ref[:, 2 * lo + LANES:2 * lo + 2 * LANES] = qaug
        k_ref[:, 2 * lo:2 * lo + LANES] = fk_ref[:, lo:lo + LANES]
        k_ref[:, 2 * lo + LANES:2 * lo + 2 * LANES] = kx[:, 2 * lo + LANES:2 * lo + 2 * LANES].astype(k_ref.dtype)


def _fox_pack(zb, cum):
    n = zb.shape[0]
    tm = min(1024, n)
    pairs = FOX_HEADS // 2
    place = np.zeros((pairs, 3 * LANES, 4 * LANES), np.float32)
    for p in range(pairs):
        for hh in range(2):
            for part in range(3):
                place[p, part * LANES + 2 * p + hh, hh * 2 * LANES + LANES + part] = 1.0
    out = jax.ShapeDtypeStruct((n, FOX_HEADS * 2 * LANES), BF16)
    ospec = pl.BlockSpec((tm, 4 * LANES), lambda i, p: (i, p))
    return pl.pallas_call(
        _fox_pack_kernel,
        grid=(n // tm, pairs),
        in_specs=[pl.BlockSpec((tm, 2 * LANES), lambda i, p: (i, p)),
                  pl.BlockSpec((tm, 2 * LANES), lambda i, p: (i, pairs + p)),
                  pl.BlockSpec((tm, LANES), lambda i, p: (i, 0)),
                  pl.BlockSpec((1, 3 * LANES, 4 * LANES), lambda i, p: (p, 0, 0))],
        out_specs=[ospec, ospec],
        out_shape=[out, out],
        compiler_params=_cparams(2),
        name="fox_pack",
    )(zb, zb, cum, jnp.asarray(place, dtype=BF16))


def _attn_kernel(q_ref, k_ref, v_ref, o_ref, vt_sc, s_sc, m_sc, acc_sc, *, tc, nsub, dv):
    i = pl.program_id(1)
    nchunks, dva, _ = vt_sc.shape

    @pl.when(i == 0)
    def _():
        ones_row = (lax.broadcasted_iota(jnp.int32, (dva - dv, tc), 0) == 0).astype(vt_sc.dtype)

        def fill(c, carry):
            off = pl.multiple_of(c * tc, tc)
            vt_sc[c, 0:dv, :] = v_ref[pl.ds(off, tc), :].astype(F32).T.astype(vt_sc.dtype)
            vt_sc[c, dv:dva, :] = ones_row
            return carry

        lax.fori_loop(0, nchunks, fill, 0)

    m_sc[...] = jnp.full_like(m_sc, NEG)
    acc_sc[...] = jnp.zeros_like(acc_sc)

    def scores(j, rs, slot):
        off = pl.multiple_of(j * tc, tc)
        k = k_ref[pl.ds(off, tc), :]
        for r in rs:
            s_sc[slot, r] = lax.dot_general(k, q_ref[r * tc:(r + 1) * tc, :],
                                            (((1,), (1,)), ((), ())), preferred_element_type=F32)

    def update(j, rs, slot, masked_r):
        for r in rs:
            s = s_sc[slot, r]
            if r == masked_r:
                key = lax.broadcasted_iota(jnp.int32, (tc, tc), 0)
                qry = lax.broadcasted_iota(jnp.int32, (tc, tc), 1)
                s = jnp.where(key <= qry, s, NEG)
            m_prev = m_sc[r]
            m_new = jnp.maximum(m_prev, jnp.max(s, axis=0, keepdims=True))
            alpha = jnp.exp2(m_prev - m_new)
            p = jnp.exp2(s - m_new).astype(vt_sc.dtype)
            acc_sc[r] = alpha * acc_sc[r] + jnp.dot(vt_sc[j], p, preferred_element_type=F32)
            m_sc[r] = m_new

    all_r = range(nsub)
    n_main = i * nsub
    scores(0, all_r, 0)

    def body(t, carry):
        j = 2 * t
        scores(j + 1, all_r, 1)
        update(j, all_r, 0, -1)
        scores(j + 2, all_r, 0)
        update(j + 1, all_r, 1, -1)
        return carry

    lax.fori_loop(0, n_main // 2, body, 0)
    for jj in range(nsub):
        if jj + 1 < nsub:
            scores(n_main + jj + 1, range(jj + 1, nsub), (jj + 1) % 2)
        update(n_main + jj, range(jj, nsub), jj % 2, jj)
    for r in range(nsub):
        acc = acc_sc[r]
        o = acc[0:dv, :] / acc[dv:dv + 1, :]
        o_ref[r * tc:(r + 1) * tc, :] = o.T.astype(o_ref.dtype)


def _attention(q_arr, k_arr, v_arr, heads, dk, dv, q_blk0, k_blk0, v_blk0, name):
    n = q_arr.shape[0]
    tc = min(ATTN_CHUNK, n)
    nsub = min(ATTN_SUBBLOCKS, n // tc)
    tq = tc * nsub
    assert nsub % 2 == 0 and n % tq == 0
    dva = dv + BF16_SUBLANES
    return pl.pallas_call(
        functools.partial(_attn_kernel, tc=tc, nsub=nsub, dv=dv),
        grid=(heads, n // tq),
        in_specs=[pl.BlockSpec((tq, dk), lambda h, i: (i, q_blk0 + h)),
                  pl.BlockSpec((n, dk), lambda h, i: (0, k_blk0 + h)),
                  pl.BlockSpec((n, dv), lambda h, i: (0, v_blk0 + h))],
        out_specs=pl.BlockSpec((tq, dv), lambda h, i: (i, h)),
        out_shape=jax.ShapeDtypeStruct((n, heads * dv), BF16),
        scratch_shapes=[pltpu.VMEM((n // tc, dva, tc), BF16),
                        pltpu.VMEM((2, nsub, tc, tc), F32),
                        pltpu.VMEM((nsub, 1, tc), F32),
                        pltpu.VMEM((nsub, dva, tc), F32)],
        compiler_params=_cparams(2),
        name=name,
    )(q_arr, k_arr, v_arr)


def _merge_kernel(ya_ref, yb_ref, wa_ref, wb_ref, ga_ref, gb_ref, o_ref):
    a = jnp.dot(ya_ref[...], wa_ref[...], preferred_element_type=F32)
    b = jnp.dot(yb_ref[...], wb_ref[...], preferred_element_type=F32)
    ga = jax.nn.sigmoid(ga_ref[...].astype(F32))
    gb = jax.nn.sigmoid(gb_ref[...].astype(F32))
    o_ref[...] = (ga * a + gb * b).astype(o_ref.dtype)


def _merge(ya, yb, w_ya, w_yb, zb, ga_col0, gb_col0, layer):
    n, ka = ya.shape
    kb = yb.shape[1]
    d = w_ya.shape[-1]
    tm, tn = min(1024, n), 512
    ga0, gb0 = ga_col0 // tn, gb_col0 // tn
    return pl.pallas_call(
        _merge_kernel,
        grid=(n // tm, d // tn),
        in_specs=[pl.BlockSpec((tm, ka), lambda i, j: (i, 0)),
                  pl.BlockSpec((tm, kb), lambda i, j: (i, 0)),
                  _wspec(layer, (ka, tn), lambda i, j: (0, j)),
                  _wspec(layer, (kb, tn), lambda i, j: (0, j)),
                  pl.BlockSpec((tm, tn), lambda i, j: (i, ga0 + j)),
                  pl.BlockSpec((tm, tn), lambda i, j: (i, gb0 + j))],
        out_specs=pl.BlockSpec((tm, tn), lambda i, j: (i, j)),
        out_shape=jax.ShapeDtypeStruct((n, d), BF16),
        compiler_params=_cparams(2),
        name="merge",
    )(ya, yb, w_ya, w_yb, zb, zb)


def _ln_kernel(*refs, alpha, n_add, with_row, with_t):
    h_ref = refs[0]
    add_refs = refs[1:1 + n_add]
    g_ref, b_ref = refs[1 + n_add:3 + n_add]
    quant = with_row or with_t
    outs = refs[3 + n_add + (1 if quant else 0):]
    x = alpha * h_ref[...]
    for r in add_refs:
        x = x + r[...].astype(F32)
    mu = jnp.mean(x, axis=-1, keepdims=True)
    xc = x - mu
    var = jnp.mean(xc * xc, axis=-1, keepdims=True)
    y = xc * lax.rsqrt(var + LN_EPS) * g_ref[...] + b_ref[...]
    outs[0][...] = y
    outs[1][...] = y.astype(outs[1].dtype)
    if quant:
        ys = y * refs[3 + n_add][...]
        k = 2
        if with_row:
            outs[k][...] = ys.astype(outs[k].dtype)
            k += 1
        if with_t:
            outs[k][...] = ys.T.astype(outs[k].dtype)


def _ln(h, addends, g, b, alpha, q_scale=None, with_row=False, with_t=False):
    n, d = h.shape
    tm = min(256, n)
    assert (q_scale is not None) == (with_row or with_t)
    row = pl.BlockSpec((tm, d), lambda i: (i, 0))
    vec = pl.BlockSpec((1, d), lambda i: (0, 0))
    in_specs = [row] * (1 + len(addends)) + [vec, vec]
    args = [h, *addends, g, b]
    out_specs = [row, row]
    out_shape = [jax.ShapeDtypeStruct((n, d), F32), jax.ShapeDtypeStruct((n, d), BF16)]
    if q_scale is not None:
        in_specs.append(pl.BlockSpec((1, 1), lambda i: (0, 0)))
        args.append(q_scale)
    if with_row:
        out_specs.append(row)
        out_shape.append(jax.ShapeDtypeStruct((n, d), F8))
    if with_t:
        out_specs.append(pl.BlockSpec((d, tm), lambda i: (0, i)))
        out_shape.append(jax.ShapeDtypeStruct((d, n), F8))
    return pl.pallas_call(
        functools.partial(_ln_kernel, alpha=alpha, n_add=len(addends), with_row=with_row, with_t=with_t),
        grid=(n // tm,),
        in_specs=in_specs,
        out_specs=out_specs,
        out_shape=out_shape,
        compiler_params=_cparams(1),
        name="deepnorm_ln",
    )(*args)


def _topk_rows(problems, k, pos=None):
    r, t = problems[0][0].shape
    iota = lax.broadcasted_iota(jnp.int32, (r, t), 0).astype(F32) if pos is None else pos
    for _, _, _, rank_ref in problems:
        if rank_ref is not None:
            rank_ref[...] = jnp.full((r, t), float(k), F32)

    def body(kk, carry):
        for x_ref, val_ref, idx_ref, rank_ref in problems:
            x = x_ref[...]
            m = jnp.max(x, axis=0, keepdims=True)
            idx = jnp.min(jnp.where(x == m, iota, POS_SENTINEL), axis=0, keepdims=True)
            sel = iota == idx
            x_ref[...] = jnp.where(sel, -jnp.inf, x)
            if rank_ref is not None:
                rank_ref[...] = jnp.where(sel, lax.convert_element_type(kk, F32), rank_ref[...])
            val_ref[pl.ds(kk, 1), :] = m
            idx_ref[pl.ds(kk, 1), :] = idx
        return carry

    lax.fori_loop(0, k, body, 0)


def _candidate_groups(k):
    assert k == 16
    return [(0, 16), (1, 8), (2, 8), (3, 8), (4, 4), (5, 4), (6, 4), (7, 4)] + [(a, 1) for a in range(8, 16)]


def _route_kernel(qp_ref, k1_ref, k2_ref, cpos_ref, nofi_ref, c_ref, r2_ref, e2_ref,
                  s_sc, cand_sc, t1_sc, i1_sc, t2_sc, i2_sc, top_sc, pos_sc, r1_sc, r2_sc):
    kk = PEER_TOPK
    qp = qp_ref[...].astype(BF16)
    nt = (((1,), (1,)), ((), ()))
    s1 = lax.dot_general(k1_ref[0].astype(BF16), qp[:, :PEER_HALF], nt, preferred_element_type=F32)
    s2 = lax.dot_general(k2_ref[0].astype(BF16), qp[:, PEER_HALF:], nt, preferred_element_type=F32)

    s_sc[0] = s1
    s_sc[1] = s2
    _topk_rows([(s_sc.at[0], t1_sc, i1_sc, r1_sc), (s_sc.at[1], t2_sc, i2_sc, r2_sc)], kk)

    row = 0
    for a, nb in _candidate_groups(kk):
        cand_sc[row:row + nb, :] = t1_sc[a:a + 1, :] + t2_sc[0:nb, :]
        row += nb
    _topk_rows([(cand_sc, top_sc, pos_sc, None)], kk, pos=cpos_ref[...])

    top = top_sc[...]
    z = jnp.sum(jnp.exp(top - top[0:1, :]), axis=0, keepdims=True)
    a_sel = jnp.floor(pos_sc[...] * (1.0 / kk))
    r1 = r1_sc[...]
    nofi = jnp.zeros_like(r1)
    for a in range(kk):
        n_a = jnp.sum((a_sel == float(a)).astype(F32), axis=0, keepdims=True)
        nofi = nofi + jnp.where(r1 == float(a), n_a, 0.0)
    nofi_ref[0] = nofi
    c_ref[0] = jnp.exp(s1 - t1_sc[0:1, :]) / z
    r2_ref[0] = r2_sc[...].astype(r2_ref.dtype)
    e2_ref[0] = jnp.exp(s2 - t2_sc[0:1, :]).astype(e2_ref.dtype)


def _route(qp, sub_k1, sub_k2):
    n = qp.shape[0]
    tr = min(512, n)
    kk = PEER_TOPK
    out = jax.ShapeDtypeStruct((PEER_HEADS, PEER_KEYS, n), F32)
    out_b = jax.ShapeDtypeStruct((PEER_HEADS, PEER_KEYS, n), BF16)
    ospec = pl.BlockSpec((1, PEER_KEYS, tr), lambda i, h: (h, 0, i))
    kspec = pl.BlockSpec((1, PEER_KEYS, PEER_HALF), lambda i, h: (h, 0, 0))
    small = pltpu.VMEM((kk, tr), F32)
    cpos = np.concatenate([a * kk + np.arange(nb) for a, nb in _candidate_groups(kk)]).astype(np.float32)
    ncand = cpos.shape[0]
    cpos = jnp.asarray(np.broadcast_to(cpos[:, None], (ncand, tr)))
    return pl.pallas_call(
        _route_kernel,
        grid=(n // tr, PEER_HEADS),
        in_specs=[pl.BlockSpec((tr, 2 * PEER_HALF), lambda i, h: (i, h)), kspec, kspec,
                  pl.BlockSpec((ncand, tr), lambda i, h: (0, 0))],
        out_specs=[ospec] * 4,
        out_shape=[out, out, out_b, out_b],
        scratch_shapes=[pltpu.VMEM((2, PEER_KEYS, tr), F32), pltpu.VMEM((ncand, tr), F32),
                        small, small, small, small, small, small,
                        pltpu.VMEM((PEER_KEYS, tr), F32), pltpu.VMEM((PEER_KEYS, tr), F32)],
        compiler_params=_cparams(2),
        name="peer_route",
    )(qp, sub_k1, sub_k2, cpos)


def _gelu_tanh(x):
    c = math.sqrt(2.0 / math.pi)
    return 0.5 * x * (1.0 + jnp.tanh(c * (x + 0.044715 * (x * x * x))))


def _gates(nofi_ref, c_ref, r2_ref, e2_ref, w_sc):
    for il in range(nofi_ref.shape[0]):
        w = None
        for h in range(PEER_HEADS):
            n_i = nofi_ref[il, h:h + 1, :].astype(BF16)
            c_i = c_ref[il, h:h + 1, :].astype(BF16)
            term = jnp.where(r2_ref[h] < n_i, e2_ref[h] * c_i, jnp.zeros((), BF16))
            w = term if w is None else w + term
        w_sc[il * PEER_KEYS:(il + 1) * PEER_KEYS, :] = w


def _expert_kernel(u_ref, vt_ref, ht_ref, inv_ref, vinv_ref, nofi0_ref, c0_ref, nofi_ref, c_ref,
                   r2_ref, e2_ref, o_ref, acc_sc, g_sc, w_sc, gs_sc):
    e = pl.program_id(1)
    old = lax.rem(e + 1, 2)
    new = lax.rem(e, 2)

    @pl.when(e == 0)
    def _():
        acc_sc[...] = jnp.zeros_like(acc_sc)
        g_sc[...] = jnp.zeros_like(g_sc)
        gs_sc[...] = jnp.ones_like(gs_sc)
        _gates(nofi0_ref, c0_ref, r2_ref, e2_ref, w_sc.at[0])

    act = jnp.dot(u_ref[...], ht_ref[...], preferred_element_type=F32) * inv_ref[...]
    acc_sc[...] += (jnp.dot(vt_ref[...], g_sc[old], preferred_element_type=F32)
                    * (vinv_ref[...] * gs_sc[old]))
    g = _gelu_tanh(act.astype(BF16)) * w_sc[new]
    amax = jnp.max(jnp.max(jnp.abs(g), axis=0, keepdims=True).astype(F32), axis=1, keepdims=True)
    g_scale = jnp.exp2(jnp.floor(jnp.log2(F8_TARGET_MAX / jnp.maximum(amax, 1e-20))))
    g_sc[new] = (g * g_scale.astype(BF16)).astype(g_sc.dtype)
    gs_sc[new] = 1.0 / g_scale
    _gates(nofi_ref, c_ref, r2_ref, e2_ref, w_sc.at[old])

    @pl.when(e == pl.num_programs(1) - 1)
    def _():
        o_ref[...] = acc_sc[...].T.astype(o_ref.dtype)


def _experts(u_q, vt_q, ht_q, inv_scale, v_inv_scale, nofi_t, c_t, r2, e2, layer=None):
    d, n = ht_q.shape
    ne = u_q.shape[-2]
    tm = min(512, n)
    te = 1024
    gi = te // PEER_KEYS
    nblk = ne // te
    cur = lambda e: jnp.minimum(e, nblk - 1)
    nxt = lambda e: jnp.minimum(e + 1, nblk - 1)
    prev = lambda e: jnp.maximum(e - 1, 0)
    ispec0 = pl.BlockSpec((gi, PEER_HEADS, tm), lambda b, e: (0, 0, b))
    ispec = pl.BlockSpec((gi, PEER_HEADS, tm), lambda b, e: (nxt(e), 0, b))
    jspec = pl.BlockSpec((PEER_HEADS, PEER_KEYS, tm), lambda b, e: (0, 0, b))
    return pl.pallas_call(
        _expert_kernel,
        grid=(n // tm, nblk + 1),
        in_specs=[_wspec(layer, (te, d), lambda b, e: (cur(e), 0)),
                  _wspec(layer, (d, te), lambda b, e: (0, prev(e))),
                  pl.BlockSpec((d, tm), lambda b, e: (0, b)),
                  pl.BlockSpec((1, 1), lambda b, e: (0, 0)),
                  pl.BlockSpec((1, 1), lambda b, e: (0, 0)),
                  ispec0, ispec0, ispec, ispec, jspec, jspec],
        out_specs=pl.BlockSpec((tm, d), lambda b, e: (b, 0)),
        out_shape=jax.ShapeDtypeStruct((n, d), BF16),
        scratch_shapes=[pltpu.VMEM((d, tm), F32), pltpu.VMEM((2, te, tm), F8),
                        pltpu.VMEM((2, te, tm), BF16), pltpu.VMEM((2, 1, 1), F32)],
        compiler_params=_cparams(2),
        name="peer_experts",
    )(u_q, vt_q, ht_q, inv_scale, v_inv_scale, nofi_t, c_t, nofi_t, c_t, r2, e2)


def _ple_kernel(h_ref, wg_ref, inv_ref, p_ref, wp_ref, o_ref):
    g = jnp.dot(h_ref[...], wg_ref[...], preferred_element_type=F32) * inv_ref[...]
    pp = jnp.dot(p_ref[...], wp_ref[...], preferred_element_type=F32)
    o_ref[...] = (jax.nn.sigmoid(g) * pp).astype(o_ref.dtype)


def _ple(h_q, w_pg_q, inv_scale, p_b, w_pe, layer):
    n, d = h_q.shape
    kp = p_b.shape[-1]
    tm, tn = min(1024, n), 1024
    return pl.pallas_call(
        _ple_kernel,
        grid=(n // tm, d // tn),
        in_specs=[pl.BlockSpec((tm, d), lambda i, j: (i, 0)),
                  _wspec(layer, (d, tn), lambda i, j: (0, j)),
                  pl.BlockSpec((1, 1), lambda i, j: (0, 0)),
                  _wspec(layer, (tm, kp), lambda i, j: (i, 0)),
                  _wspec(layer, (kp, tn), lambda i, j: (0, j))],
        out_specs=pl.BlockSpec((tm, tn), lambda i, j: (i, j)),
        out_shape=jax.ShapeDtypeStruct((n, d), BF16),
        compiler_params=_cparams(2),
        name="ple",
    )(h_q, w_pg_q, inv_scale, p_b, w_pe)


def _pow2_scale(amax):
    return jnp.exp2(jnp.floor(jnp.log2(F8_TARGET_MAX / jnp.maximum(amax, 1e-30))))


def _ln_out_scale(g, b, d):
    return _pow2_scale(math.sqrt(d) * jnp.max(jnp.abs(g), axis=1) + jnp.max(jnp.abs(b), axis=1))


def _mixer_weights(w_in, w_uq, w_ukv, b_f):
    depth, d, _ = w_in.shape
    sizes = (MLA_Q_RANK, MLA_KV_RANK, MLA_ROPE, FOX_HEADS * FOX_DIM, FOX_HEADS * FOX_DIM,
             FOX_HEADS * FOX_DIM, FOX_HEADS, d, d)
    offs = np.concatenate([[0], np.cumsum(sizes)])
    cq, ckv, kr, _, _, _, fl, _, _ = [w_in[:, :, offs[t]:offs[t + 1]] for t in range(9)]
    zpad = lambda c: jnp.zeros((depth, d, c), w_in.dtype)
    w_small = jnp.concatenate([cq, ckv, kr, zpad(LANES - MLA_ROPE), fl, zpad(LANES - FOX_HEADS)],
                              axis=2).astype(BF16)
    w_fox = w_in[:, :, offs[3]:offs[6]].astype(BF16)
    w_gate = w_in[:, :, offs[7]:offs[9]]
    gate_scale = _pow2_scale(jnp.max(jnp.abs(w_gate), axis=(1, 2)))
    w_gate_q = (w_gate * gate_scale[:, None, None]).astype(F8)
    colscale = np.ones((1, w_fox.shape[2]), np.float32)
    colscale[0, :FOX_HEADS * FOX_DIM] = FOX_DIM ** -0.5 * LOG2E
    qk = MLA_NOPE + MLA_ROPE
    wq3 = w_uq.reshape(depth, MLA_Q_RANK, MLA_HEADS, qk)
    wqn = wq3[..., :MLA_NOPE].reshape(depth, MLA_Q_RANK, MLA_HEADS * MLA_NOPE).astype(BF16)
    wqr = jnp.pad(wq3[..., MLA_NOPE:], ((0, 0), (0, 0), (0, 0), (0, LANES - MLA_ROPE))
                  ).reshape(depth, MLA_Q_RANK, MLA_HEADS * LANES).astype(BF16)
    wkv3 = w_ukv.reshape(depth, MLA_KV_RANK, MLA_HEADS, MLA_NOPE + MLA_V)
    wuk = wkv3[..., :MLA_NOPE].reshape(depth, MLA_KV_RANK, MLA_HEADS * MLA_NOPE).astype(BF16)
    wuv = wkv3[..., MLA_NOPE:].reshape(depth, MLA_KV_RANK, MLA_HEADS * MLA_V).astype(BF16)
    bf_rows = jnp.pad(b_f, ((0, 0), (0, LANES - FOX_HEADS))).reshape(depth, 1, LANES)
    return w_small, w_fox, w_gate_q, gate_scale, jnp.asarray(colscale), wqn, wqr, wuk, wuv, bf_rows


def kernel(x, p, positions, w_in, g_cq, g_ckv, w_uq, w_ukv, b_f, w_ya, w_yb, w_o, ln1_g, ln1_b,
           w_pq, sub_k1, sub_k2, u_tab, v_tab, w_pg, w_pe, ln2_g, ln2_b):
    bsz, seq, d = x.shape
    depth = w_in.shape[0]
    n = bsz * seq
    assert bsz == 1, "token-major layout assumes a single sequence"
    alpha = (2 * depth) ** 0.25

    h = x.reshape(n, d)
    h_b = h.astype(BF16)
    tabs = _rope_tables(positions.reshape(n, 1))

    (w_small, w_fox, w_gate_q, gate_scale, colscale, wqn, wqr, wuk, wuv, bf_rows) = _mixer_weights(
        w_in, w_uq, w_ukv, b_f)
    w_ya_b, w_yb_b, w_o_b = w_ya.astype(BF16), w_yb.astype(BF16), w_o.astype(BF16)
    w_pq_b, w_pe_b = w_pq.astype(BF16), w_pe.astype(BF16)
    v_scale = _pow2_scale(jnp.max(jnp.abs(v_tab), axis=(1, 2)))
    vt_q = (v_tab.transpose(0, 2, 1) * v_scale[:, None, None]).astype(F8)
    p_b = p.reshape(depth, n, -1).astype(BF16)
    u_scale = _pow2_scale(jnp.max(jnp.abs(u_tab), axis=(1, 2)))
    u_q = (u_tab * u_scale[:, None, None]).astype(F8)
    pg_scale = _pow2_scale(jnp.max(jnp.abs(w_pg), axis=(1, 2)))
    w_pg_q = (w_pg * pg_scale[:, None, None]).astype(F8)
    h1_scale = _ln_out_scale(ln1_g, ln1_b, d)
    h2_scale = _ln_out_scale(ln2_g, ln2_b, d)
    in_scale = _pow2_scale(jnp.max(jnp.abs(h)))
    h_q = (h * in_scale).astype(F8)

    for i in range(depth):
        zs = _mm(h_b, w_small, F32, 1024, 256, name="in_proj_small", layer=i)
        zf = _mm(h_b, w_fox, BF16, 1024, 1024, colscale, name="in_proj_fox", layer=i)
        gate_inv = jnp.full((1, 2 * d), 1.0, F32) / (in_scale * gate_scale[i])
        zg = _mm(h_q, w_gate_q, BF16, 1024, 1024, gate_inv, name="in_proj_gate", layer=i)
        q_mla = _mla_q(zs, g_cq[i].reshape(1, -1), wqn[i], wqr[i], tabs)
        k_mla, v_mla = _mla_kv(zs, g_ckv[i].reshape(1, -1), wuk[i], wuv[i], tabs)
        y_a = _attention(q_mla, k_mla, v_mla, MLA_HEADS, 2 * LANES, MLA_V, 0, 0, 0, "attn_mla")
        q_fox, k_fox = _fox_pack(zf, _fox_cum(zs, bf_rows[i]))
        y_b = _attention(q_fox, k_fox, zf, FOX_HEADS, 2 * LANES, FOX_DIM, 0, 0, 2 * FOX_HEADS,
                         "attn_fox")
        merged = _merge(y_a, y_b, w_ya_b, w_yb_b, zg, 0, d, i)
        mix = _mm(merged, w_o_b, BF16, 1024, 1024, name="out_proj", layer=i)
        h, h_b, h1_q, ht_q = _ln(h, [mix], ln1_g[i].reshape(1, d), ln1_b[i].reshape(1, d), alpha,
                                 h1_scale[i].reshape(1, 1), with_row=True, with_t=True)

        qp = _mm(h_b, w_pq_b, F32, 1024, 1024, name="peer_query", layer=i)
        nofi, c, r2, e2 = _route(qp, sub_k1[i], sub_k2[i])
        ffn = _experts(u_q, vt_q, ht_q, (1.0 / (u_scale[i] * h1_scale[i])).reshape(1, 1),
                       (1.0 / v_scale[i]).reshape(1, 1),
                       nofi.transpose(1, 0, 2), c.transpose(1, 0, 2), r2, e2, i)
        ple = _ple(h1_q, w_pg_q, (1.0 / (h1_scale[i] * pg_scale[i])).reshape(1, 1), p_b, w_pe_b, i)
        if i + 1 < depth:
            in_scale = h2_scale[i]
            h, h_b, h_q = _ln(h, [ffn, ple], ln2_g[i].reshape(1, d), ln2_b[i].reshape(1, d), alpha,
                              in_scale.reshape(1, 1), with_row=True)
        else:
            h, h_b = _ln(h, [ffn, ple], ln2_g[i].reshape(1, d), ln2_b[i].reshape(1, d), alpha)

    return h.reshape(bsz, seq, d)
```

```python
import functools
import math

import numpy as np
import jax
import jax.numpy as jnp
from jax import lax
from jax.experimental import pallas as pl
from jax.experimental.pallas import tpu as pltpu

F32 = jnp.float32
BF16 = jnp.bfloat16
F8 = jnp.float8_e4m3fn
F8_TARGET_MAX = 240.0

MLA_HEADS = 16
MLA_Q_RANK = 1024
MLA_KV_RANK = 512
MLA_NOPE = 128
MLA_ROPE = 64
MLA_V = 128
ROPE_THETA = 10000.0
FOX_HEADS = 16
FOX_DIM = 128
PEER_HEADS = 8
PEER_KEYS = 128
PEER_HALF = 128
PEER_TOPK = 16
LN_EPS = 1e-5
RMS_EPS = 1e-6
LANES = 128
BF16_SUBLANES = 16
LOG2E = math.log2(math.e)
NEG = -1e30
POS_SENTINEL = 1e9
MLA_HEAD_GROUP = 4
ATTN_CHUNK = 1024
ATTN_SUBBLOCKS = 2

VMEM_LIMIT = 56 * 1024 * 1024


def _cparams(n_axes):
    return pltpu.CompilerParams(dimension_semantics=("arbitrary",) * n_axes,
                                vmem_limit_bytes=VMEM_LIMIT)


def _mm_kernel(x_ref, w_ref, o_ref):
    o_ref[...] = jnp.dot(x_ref[...], w_ref[...], preferred_element_type=F32).astype(o_ref.dtype)


def _mm_scale_kernel(x_ref, w_ref, s_ref, o_ref):
    acc = jnp.dot(x_ref[...], w_ref[...], preferred_element_type=F32)
    o_ref[...] = (acc * s_ref[...]).astype(o_ref.dtype)


def _wspec(layer, block, index_map):
    if layer is None:
        return pl.BlockSpec(block, index_map)
    return pl.BlockSpec((None,) + block, lambda *g: (layer,) + index_map(*g))


def _mm(x, w, out_dtype, tm, tn, colscale=None, name="mm", layer=None):
    m, k = x.shape
    n = w.shape[-1]
    tm, tn = min(tm, m), min(tn, n)
    assert m % tm == 0 and n % tn == 0
    in_specs = [pl.BlockSpec((tm, k), lambda i, j: (i, 0)),
                _wspec(layer, (k, tn), lambda i, j: (0, j))]
    args = [x, w]
    body = _mm_kernel
    if colscale is not None:
        in_specs.append(pl.BlockSpec((1, tn), lambda i, j: (0, j)))
        args.append(colscale)
        body = _mm_scale_kernel
    return pl.pallas_call(
        body,
        grid=(m // tm, n // tn),
        in_specs=in_specs,
        out_specs=pl.BlockSpec((tm, tn), lambda i, j: (i, j)),
        out_shape=jax.ShapeDtypeStruct((m, n), out_dtype),
        compiler_params=_cparams(2),
        name=name,
    )(*args)


def _rope_table_kernel(pos_ref, c_ref, cos_ref, sina_ref, sinb_ref):
    ang = pos_ref[...].astype(F32) * c_ref[0:1, :]
    cos = jnp.cos(ang)
    sin = jnp.sin(ang)
    cos_ref[...] = cos * c_ref[1:2, :]
    sina_ref[...] = sin * c_ref[2:3, :]
    sinb_ref[...] = sin * c_ref[3:4, :]


def _rope_tables(positions_col):
    n = positions_col.shape[0]
    tm = min(1024, n)
    half = MLA_ROPE // 2
    inv_freq = (ROPE_THETA ** (-np.arange(half, dtype=np.float32) / half)).astype(np.float32)
    consts = np.zeros((8, LANES), np.float32)
    consts[0, :half] = inv_freq
    consts[0, half:2 * half] = inv_freq
    consts[1, :2 * half] = 1.0
    consts[2, :half] = -1.0
    consts[3, half:2 * half] = 1.0
    shp = jax.ShapeDtypeStruct((n, LANES), F32)
    return pl.pallas_call(
        _rope_table_kernel,
        grid=(n // tm,),
        in_specs=[pl.BlockSpec((tm, 1), lambda i: (i, 0)),
                  pl.BlockSpec((8, LANES), lambda i: (0, 0))],
        out_specs=[pl.BlockSpec((tm, LANES), lambda i: (i, 0))] * 3,
        out_shape=[shp, shp, shp],
        compiler_params=_cparams(1),
        name="rope_tables",
    )(positions_col, jnp.asarray(consts))


def _rope128(x, cos, sina, sinb):
    return (x * cos + pltpu.roll(x, 3 * (MLA_ROPE // 2), 1) * sina
            + pltpu.roll(x, MLA_ROPE // 2, 1) * sinb)


def _rms(x, g):
    return x * lax.rsqrt(jnp.mean(x * x, axis=-1, keepdims=True) + RMS_EPS) * g


def _mla_q_kernel(cq_ref, g_ref, wn_ref, wr_ref, cos_ref, sina_ref, sinb_ref, q_ref, *, scale):
    xn = _rms(cq_ref[...], g_ref[...]).astype(BF16)
    qn = jnp.dot(xn, wn_ref[...], preferred_element_type=F32) * scale
    qr = jnp.dot(xn, wr_ref[...], preferred_element_type=F32) * scale
    cos, sina, sinb = cos_ref[...], sina_ref[...], sinb_ref[...]
    for hh in range(wn_ref.shape[1] // LANES):
        lo = hh * LANES
        q_ref[:, 2 * lo:2 * lo + LANES] = qn[:, lo:lo + LANES].astype(q_ref.dtype)
        q_ref[:, 2 * lo + LANES:2 * lo + 2 * LANES] = _rope128(
            qr[:, lo:lo + LANES], cos, sina, sinb).astype(q_ref.dtype)


def _mla_q(zs, g_cq, wqn, wqr, tabs):
    n = zs.shape[0]
    tm = min(1024, n)
    grp = MLA_HEAD_GROUP
    scale = (MLA_NOPE + MLA_ROPE) ** -0.5 * LOG2E
    tab_spec = pl.BlockSpec((tm, LANES), lambda i, p: (i, 0))
    return pl.pallas_call(
        functools.partial(_mla_q_kernel, scale=scale),
        grid=(n // tm, MLA_HEADS // grp),
        in_specs=[pl.BlockSpec((tm, MLA_Q_RANK), lambda i, p: (i, 0)),
                  pl.BlockSpec((1, MLA_Q_RANK), lambda i, p: (0, 0)),
                  pl.BlockSpec((MLA_Q_RANK, grp * LANES), lambda i, p: (0, p)),
                  pl.BlockSpec((MLA_Q_RANK, grp * LANES), lambda i, p: (0, p)),
                  tab_spec, tab_spec, tab_spec],
        out_specs=pl.BlockSpec((tm, grp * 2 * LANES), lambda i, p: (i, p)),
        out_shape=jax.ShapeDtypeStruct((n, MLA_HEADS * 2 * LANES), BF16),
        compiler_params=_cparams(2),
        name="mla_q",
    )(zs, g_cq, wqn, wqr, *tabs)


def _mla_kv_kernel(ckv_ref, g_ref, kr_ref, wk_ref, wv_ref, cos_ref, sina_ref, sinb_ref,
                   k_ref, v_ref):
    xn = _rms(ckv_ref[...], g_ref[...]).astype(BF16)
    kn = jnp.dot(xn, wk_ref[...], preferred_element_type=F32)
    v_ref[...] = jnp.dot(xn, wv_ref[...], preferred_element_type=F32).astype(v_ref.dtype)
    kr = _rope128(kr_ref[...], cos_ref[...], sina_ref[...], sinb_ref[...]).astype(k_ref.dtype)
    for hh in range(wk_ref.shape[1] // LANES):
        lo = hh * LANES
        k_ref[:, 2 * lo:2 * lo + LANES] = kn[:, lo:lo + LANES].astype(k_ref.dtype)
        k_ref[:, 2 * lo + LANES:2 * lo + 2 * LANES] = kr


def _mla_kv(zs, g_ckv, wuk, wuv, tabs):
    n = zs.shape[0]
    tm = min(1024, n)
    grp = MLA_HEAD_GROUP
    ckv_blk = MLA_Q_RANK // MLA_KV_RANK
    kr_blk = (MLA_Q_RANK + MLA_KV_RANK) // LANES
    tab_spec = pl.BlockSpec((tm, LANES), lambda i, p: (i, 0))
    return pl.pallas_call(
        _mla_kv_kernel,
        grid=(n // tm, MLA_HEADS // grp),
        in_specs=[pl.BlockSpec((tm, MLA_KV_RANK), lambda i, p: (i, ckv_blk)),
                  pl.BlockSpec((1, MLA_KV_RANK), lambda i, p: (0, 0)),
                  pl.BlockSpec((tm, LANES), lambda i, p: (i, kr_blk)),
                  pl.BlockSpec((MLA_KV_RANK, grp * LANES), lambda i, p: (0, p)),
                  pl.BlockSpec((MLA_KV_RANK, grp * LANES), lambda i, p: (0, p)),
                  tab_spec, tab_spec, tab_spec],
        out_specs=[pl.BlockSpec((tm, grp * 2 * LANES), lambda i, p: (i, p)),
                   pl.BlockSpec((tm, grp * LANES), lambda i, p: (i, p))],
        out_shape=[jax.ShapeDtypeStruct((n, MLA_HEADS * 2 * LANES), BF16),
                   jax.ShapeDtypeStruct((n, MLA_HEADS * MLA_V), BF16)],
        compiler_params=_cparams(2),
        name="mla_kv",
    )(zs, g_ckv, zs, wuk, wuv, *tabs)


def _split3(x):
    hi = x.astype(BF16)
    r = x - hi.astype(F32)
    mid = r.astype(BF16)
    lo = (r - mid.astype(F32)).astype(BF16)
    return hi, mid, lo


def _cum_kernel(fl_ref, bf_ref, tri_ref, o_ref, carry_ref):
    @pl.when(pl.program_id(0) == 0)
    def _():
        carry_ref[...] = jnp.zeros_like(carry_ref)

    x = fl_ref[...] + bf_ref[...]
    lf = jnp.minimum(x, 0.0) - jnp.log1p(jnp.exp(-jnp.abs(x)))
    tri = tri_ref[...]
    c = carry_ref[...]
    for part in _split3(lf):
        c = c + jnp.dot(tri, part, preferred_element_type=F32)
    carry_ref[...] = c[c.shape[0] - 1:, :]
    o_ref[...] = c


def _fox_cum(zs, bf_row):
    n = zs.shape[0]
    tm = 256
    fl_blk = (MLA_Q_RANK + MLA_KV_RANK) // LANES + 1
    tri = jnp.asarray(np.tril(np.ones((tm, tm), np.float32)), dtype=BF16)
    return pl.pallas_call(
        _cum_kernel,
        grid=(n // tm,),
        in_specs=[pl.BlockSpec((tm, LANES), lambda i: (i, fl_blk)),
                  pl.BlockSpec((1, LANES), lambda i: (0, 0)),
                  pl.BlockSpec((tm, tm), lambda i: (0, 0))],
        out_specs=pl.BlockSpec((tm, LANES), lambda i: (i, 0)),
        out_shape=jax.ShapeDtypeStruct((n, LANES), F32),
        scratch_shapes=[pltpu.VMEM((1, LANES), F32)],
        compiler_params=_cparams(1),
        name="fox_cum",
    )(zs, bf_row, tri)


def _fox_pack_kernel(fq_ref, fk_ref, cum_ref, e_ref, q_ref, k_ref):
    c3 = jnp.concatenate(_split3(cum_ref[...] * LOG2E), axis=1)
    kx = jnp.dot(c3, e_ref[0], preferred_element_type=F32)
    lane = lax.broadcasted_iota(jnp.int32, (fq_ref.shape[0], LANES), 1)
    qaug = jnp.where(lane < 3, -1.0, 0.0).astype(q_ref.dtype)
    for hh in range(2):
        lo = hh * LANES
        q_ref[:, 2 * lo:2 * lo + LANES] = fq_ref[:, lo:lo + LANES]
        q_ref[:, 2 * lo + LANES:2 * lo + 2 * LANES] = qaug
        k_ref[:, 2 * lo:2 * lo + LANES] = fk_ref[:, lo:lo + LANES]
        k_ref[:, 2 * lo + LANES:2 * lo + 2 * LANES] = kx[:, 2 * lo + LANES:2 * lo + 2 * LANES].astype(k_ref.dtype)


def _fox_pack(zb, cum):
    n = zb.shape[0]
    tm = min(1024, n)
    pairs = FOX_HEADS // 2
    place = np.zeros((pairs, 3 * LANES, 4 * LANES), np.float32)
    for p in range(pairs):
        for hh in range(2):
            for part in range(3):
                place[p, part * LANES + 2 * p + hh, hh * 2 * LANES + LANES + part] = 1.0
    out = jax.ShapeDtypeStruct((n, FOX_HEADS * 2 * LANES), BF16)
    ospec = pl.BlockSpec((tm, 4 * LANES), lambda i, p: (i, p))
    return pl.pallas_call(
        _fox_pack_kernel,
        grid=(n // tm, pairs),
        in_specs=[pl.BlockSpec((tm, 2 * LANES), lambda i, p: (i, p)),
                  pl.BlockSpec((tm, 2 * LANES), lambda i, p: (i, pairs + p)),
                  pl.BlockSpec((tm, LANES), lambda i, p: (i, 0)),
                  pl.BlockSpec((1, 3 * LANES, 4 * LANES), lambda i, p: (p, 0, 0))],
        out_specs=[ospec, ospec],
        out_shape=[out, out],
        compiler_params=_cparams(2),
        name="fox_pack",
    )(zb, zb, cum, jnp.asarray(place, dtype=BF16))


def _attn_kernel(q_ref, k_ref, v_ref, o_ref, vt_sc, s_sc, m_sc, acc_sc, *, tc, nsub, dv):
    i = pl.program_id(1)
    nchunks, dva, _ = vt_sc.shape

    @pl.when(i == 0)
    def _():
        ones_row = (lax.broadcasted_iota(jnp.int32, (dva - dv, tc), 0) == 0).astype(vt_sc.dtype)

        def fill(c, carry):
            off = pl.multiple_of(c * tc, tc)
            vt_sc[c, 0:dv, :] = v_ref[pl.ds(off, tc), :].astype(F32).T.astype(vt_sc.dtype)
            vt_sc[c, dv:dva, :] = ones_row
            return carry

        lax.fori_loop(0, nchunks, fill, 0)

    m_sc[...] = jnp.full_like(m_sc, NEG)
    acc_sc[...] = jnp.zeros_like(acc_sc)

    def scores(j, rs, slot):
        off = pl.multiple_of(j * tc, tc)
        k = k_ref[pl.ds(off, tc), :]
        for r in rs:
            s_sc[slot, r] = lax.dot_general(k, q_ref[r * tc:(r + 1) * tc, :],
                                            (((1,), (1,)), ((), ())), preferred_element_type=F32)

    def update(j, rs, slot, masked_r):
        for r in rs:
            s = s_sc[slot, r]
            if r == masked_r:
                key = lax.broadcasted_iota(jnp.int32, (tc, tc), 0)
                qry = lax.broadcasted_iota(jnp.int32, (tc, tc), 1)
                s = jnp.where(key <= qry, s, NEG)
            m_prev = m_sc[r]
            m_new = jnp.maximum(m_prev, jnp.max(s, axis=0, keepdims=True))
            alpha = jnp.exp2(m_prev - m_new)
            p = jnp.exp2(s - m_new).astype(vt_sc.dtype)
            acc_sc[r] = alpha * acc_sc[r] + jnp.dot(vt_sc[j], p, preferred_element_type=F32)
            m_sc[r] = m_new

    all_r = range(nsub)
    n_main = i * nsub
    scores(0, all_r, 0)

    def body(t, carry):
        j = 2 * t
        scores(j + 1, all_r, 1)
        update(j, all_r, 0, -1)
        scores(j + 2, all_r, 0)
        update(j + 1, all_r, 1, -1)
        return carry

    lax.fori_loop(0, n_main // 2, body, 0)
    for jj in range(nsub):
        if jj + 1 < nsub:
            scores(n_main + jj + 1, range(jj + 1, nsub), (jj + 1) % 2)
        update(n_main + jj, range(jj, nsub), jj % 2, jj)
    for r in range(nsub):
        acc = acc_sc[r]
        o = acc[0:dv, :] / acc[dv:dv + 1, :]
        o_ref[r * tc:(r + 1) * tc, :] = o.T.astype(o_ref.dtype)


def _attention(q_arr, k_arr, v_arr, heads, dk, dv, q_blk0, k_blk0, v_blk0, name):
    n = q_arr.shape[0]
    tc = min(ATTN_CHUNK, n)
    nsub = min(ATTN_SUBBLOCKS, n // tc)
    tq = tc * nsub
    assert nsub % 2 == 0 and n % tq == 0
    dva = dv + BF16_SUBLANES
    return pl.pallas_call(
        functools.partial(_attn_kernel, tc=tc, nsub=nsub, dv=dv),
        grid=(heads, n // tq),
        in_specs=[pl.BlockSpec((tq, dk), lambda h, i: (i, q_blk0 + h)),
                  pl.BlockSpec((n, dk), lambda h, i: (0, k_blk0 + h)),
                  pl.BlockSpec((n, dv), lambda h, i: (0, v_blk0 + h))],
        out_specs=pl.BlockSpec((tq, dv), lambda h, i: (i, h)),
        out_shape=jax.ShapeDtypeStruct((n, heads * dv), BF16),
        scratch_shapes=[pltpu.VMEM((n // tc, dva, tc), BF16),
                        pltpu.VMEM((2, nsub, tc, tc), F32),
                        pltpu.VMEM((nsub, 1, tc), F32),
                        pltpu.VMEM((nsub, dva, tc), F32)],
        compiler_params=_cparams(2),
        name=name,
    )(q_arr, k_arr, v_arr)


def _merge_kernel(ya_ref, yb_ref, wa_ref, wb_ref, ga_ref, gb_ref, o_ref):
    a = jnp.dot(ya_ref[...], wa_ref[...], preferred_element_type=F32)
    b = jnp.dot(yb_ref[...], wb_ref[...], preferred_element_type=F32)
    ga = jax.nn.sigmoid(ga_ref[...].astype(F32))
    gb = jax.nn.sigmoid(gb_ref[...].astype(F32))
    o_ref[...] = (ga * a + gb * b).astype(o_ref.dtype)


def _merge(ya, yb, w_ya, w_yb, zb, ga_col0, gb_col0, layer):
    n, ka = ya.shape
    kb = yb.shape[1]
    d = w_ya.shape[-1]
    tm, tn = min(1024, n), 1024
    ga0, gb0 = ga_col0 // tn, gb_col0 // tn
    return pl.pallas_call(
        _merge_kernel,
        grid=(n // tm, d // tn),
        in_specs=[pl.BlockSpec((tm, ka), lambda i, j: (i, 0)),
                  pl.BlockSpec((tm, kb), lambda i, j: (i, 0)),
                  _wspec(layer, (ka, tn), lambda i, j: (0, j)),
                  _wspec(layer, (kb, tn), lambda i, j: (0, j)),
                  pl.BlockSpec((tm, tn), lambda i, j: (i, ga0 + j)),
                  pl.BlockSpec((tm, tn), lambda i, j: (i, gb0 + j))],
        out_specs=pl.BlockSpec((tm, tn), lambda i, j: (i, j)),
        out_shape=jax.ShapeDtypeStruct((n, d), BF16),
        compiler_params=_cparams(2),
        name="merge",
    )(ya, yb, w_ya, w_yb, zb, zb)


def _ln_kernel(*refs, alpha, n_add, with_row, with_t):
    h_ref = refs[0]
    add_refs = refs[1:1 + n_add]
    g_ref, b_ref = refs[1 + n_add:3 + n_add]
    quant = with_row or with_t
    outs = refs[3 + n_add + (1 if quant else 0):]
    x = alpha * h_ref[...]
    for r in add_refs:
        x = x + r[...].astype(F32)
    mu = jnp.mean(x, axis=-1, keepdims=True)
    xc = x - mu
    var = jnp.mean(xc * xc, axis=-1, keepdims=True)
    y = xc * lax.rsqrt(var + LN_EPS) * g_ref[...] + b_ref[...]
    outs[0][...] = y
    outs[1][...] = y.astype(outs[1].dtype)
    if quant:
        ys = y * refs[3 + n_add][...]
        k = 2
        if with_row:
            outs[k][...] = ys.astype(outs[k].dtype)
            k += 1
        if with_t:
            outs[k][...] = ys.T.astype(outs[k].dtype)


def _ln(h, addends, g, b, alpha, q_scale=None, with_row=False, with_t=False):
    n, d = h.shape
    tm = min(256, n)
    assert (q_scale is not None) == (with_row or with_t)
    row = pl.BlockSpec((tm, d), lambda i: (i, 0))
    vec = pl.BlockSpec((1, d), lambda i: (0, 0))
    in_specs = [row] * (1 + len(addends)) + [vec, vec]
    args = [h, *addends, g, b]
    out_specs = [row, row]
    out_shape = [jax.ShapeDtypeStruct((n, d), F32), jax.ShapeDtypeStruct((n, d), BF16)]
    if q_scale is not None:
        in_specs.append(pl.BlockSpec((1, 1), lambda i: (0, 0)))
        args.append(q_scale)
    if with_row:
        out_specs.append(row)
        out_shape.append(jax.ShapeDtypeStruct((n, d), F8))
    if with_t:
        out_specs.append(pl.BlockSpec((d, tm), lambda i: (0, i)))
        out_shape.append(jax.ShapeDtypeStruct((d, n), F8))
    return pl.pallas_call(
        functools.partial(_ln_kernel, alpha=alpha, n_add=len(addends), with_row=with_row, with_t=with_t),
        grid=(n // tm,),
        in_specs=in_specs,
        out_specs=out_specs,
        out_shape=out_shape,
        compiler_params=_cparams(1),
        name="deepnorm_ln",
    )(*args)


def _topk_rows(problems, k, pos=None):
    r, t = problems[0][0].shape
    iota = lax.broadcasted_iota(jnp.int32, (r, t), 0).astype(F32) if pos is None else pos
    for _, _, _, rank_ref in problems:
        if rank_ref is not None:
            rank_ref[...] = jnp.full((r, t), float(k), F32)

    def body(kk, carry):
        for x_ref, val_ref, idx_ref, rank_ref in problems:
            x = x_ref[...]
            m = jnp.max(x, axis=0, keepdims=True)
            idx = jnp.min(jnp.where(x == m, iota, POS_SENTINEL), axis=0, keepdims=True)
            sel = iota == idx
            x_ref[...] = jnp.where(sel, -jnp.inf, x)
            if rank_ref is not None:
                rank_ref[...] = jnp.where(sel, lax.convert_element_type(kk, F32), rank_ref[...])
            val_ref[pl.ds(kk, 1), :] = m
            idx_ref[pl.ds(kk, 1), :] = idx
        return carry

    lax.fori_loop(0, k, body, 0)


def _candidate_groups(k):
    assert k == 16
    return [(0, 16), (1, 8), (2, 8), (3, 8), (4, 4), (5, 4), (6, 4), (7, 4)] + [(a, 1) for a in range(8, 16)]


def _route_kernel(qp_ref, k1_ref, k2_ref, cpos_ref, nofi_ref, c_ref, r2_ref, e2_ref,
                  s_sc, cand_sc, t1_sc, i1_sc, t2_sc, i2_sc, top_sc, pos_sc, r1_sc, r2_sc):
    kk = PEER_TOPK
    qp = qp_ref[...].astype(BF16)
    nt = (((1,), (1,)), ((), ()))
    s1 = lax.dot_general(k1_ref[0].astype(BF16), qp[:, :PEER_HALF], nt, preferred_element_type=F32)
    s2 = lax.dot_general(k2_ref[0].astype(BF16), qp[:, PEER_HALF:], nt, preferred_element_type=F32)

    s_sc[0] = s1
    s_sc[1] = s2
    _topk_rows([(s_sc.at[0], t1_sc, i1_sc, r1_sc), (s_sc.at[1], t2_sc, i2_sc, r2_sc)], kk)

    row = 0
    for a, nb in _candidate_groups(kk):
        cand_sc[row:row + nb, :] = t1_sc[a:a + 1, :] + t2_sc[0:nb, :]
        row += nb
    _topk_rows([(cand_sc, top_sc, pos_sc, None)], kk, pos=cpos_ref[...])

    top = top_sc[...]
    z = jnp.sum(jnp.exp(top - top[0:1, :]), axis=0, keepdims=True)
    a_sel = jnp.floor(pos_sc[...] * (1.0 / kk))
    r1 = r1_sc[...]
    nofi = jnp.zeros_like(r1)
    for a in range(kk):
        n_a = jnp.sum((a_sel == float(a)).astype(F32), axis=0, keepdims=True)
        nofi = nofi + jnp.where(r1 == float(a), n_a, 0.0)
    nofi_ref[0] = nofi
    c_ref[0] = jnp.exp(s1 - t1_sc[0:1, :]) / z
    r2_ref[0] = r2_sc[...].astype(r2_ref.dtype)
    e2_ref[0] = jnp.exp(s2 - t2_sc[0:1, :]).astype(e2_ref.dtype)


def _route(qp, sub_k1, sub_k2):
    n = qp.shape[0]
    tr = min(512, n)
    kk = PEER_TOPK
    out = jax.ShapeDtypeStruct((PEER_HEADS, PEER_KEYS, n), F32)
    out_b = jax.ShapeDtypeStruct((PEER_HEADS, PEER_KEYS, n), BF16)
    ospec = pl.BlockSpec((1, PEER_KEYS, tr), lambda i, h: (h, 0, i))
    kspec = pl.BlockSpec((1, PEER_KEYS, PEER_HALF), lambda i, h: (h, 0, 0))
    small = pltpu.VMEM((kk, tr), F32)
    cpos = np.concatenate([a * kk + np.arange(nb) for a, nb in _candidate_groups(kk)]).astype(np.float32)
    ncand = cpos.shape[0]
    cpos = jnp.asarray(np.broadcast_to(cpos[:, None], (ncand, tr)))
    return pl.pallas_call(
        _route_kernel,
        grid=(n // tr, PEER_HEADS),
        in_specs=[pl.BlockSpec((tr, 2 * PEER_HALF), lambda i, h: (i, h)), kspec, kspec,
                  pl.BlockSpec((ncand, tr), lambda i, h: (0, 0))],
        out_specs=[ospec] * 4,
        out_shape=[out, out, out_b, out_b],
        scratch_shapes=[pltpu.VMEM((2, PEER_KEYS, tr), F32), pltpu.VMEM((ncand, tr), F32),
                        small, small, small, small, small, small,
                        pltpu.VMEM((PEER_KEYS, tr), F32), pltpu.VMEM((PEER_KEYS, tr), F32)],
        compiler_params=_cparams(2),
        name="peer_route",
    )(qp, sub_k1, sub_k2, cpos)


def _gelu_tanh(x):
    c = math.sqrt(2.0 / math.pi)
    return 0.5 * x * (1.0 + jnp.tanh(c * (x + 0.044715 * (x * x * x))))


def _gates(nofi_ref, c_ref, r2_ref, e2_ref, w_sc):
    for il in range(nofi_ref.shape[0]):
        w = None
        for h in range(PEER_HEADS):
            n_i = nofi_ref[il, h:h + 1, :].astype(BF16)
            c_i = c_ref[il, h:h + 1, :].astype(BF16)
            term = jnp.where(r2_ref[h] < n_i, e2_ref[h] * c_i, jnp.zeros((), BF16))
            w = term if w is None else w + term
        w_sc[il * PEER_KEYS:(il + 1) * PEER_KEYS, :] = w


def _expert_kernel(u_ref, vt_ref, ht_ref, inv_ref, vinv_ref, nofi0_ref, c0_ref, nofi_ref, c_ref,
                   r2_ref, e2_ref, o_ref, acc_sc, g_sc, w_sc, gs_sc):
    e = pl.program_id(1)
    old = lax.rem(e + 1, 2)
    new = lax.rem(e, 2)

    @pl.when(e == 0)
    def _():
        acc_sc[...] = jnp.zeros_like(acc_sc)
        g_sc[...] = jnp.zeros_like(g_sc)
        gs_sc[...] = jnp.ones_like(gs_sc)
        _gates(nofi0_ref, c0_ref, r2_ref, e2_ref, w_sc.at[0])

    act = jnp.dot(u_ref[...], ht_ref[...], preferred_element_type=F32) * inv_ref[...]
    acc_sc[...] += (jnp.dot(vt_ref[...], g_sc[old], preferred_element_type=F32)
                    * (vinv_ref[...] * gs_sc[old]))
    g = _gelu_tanh(act.astype(BF16)) * w_sc[new]
    amax = jnp.max(jnp.max(jnp.abs(g), axis=0, keepdims=True).astype(F32), axis=1, keepdims=True)
    g_scale = jnp.exp2(jnp.floor(jnp.log2(F8_TARGET_MAX / jnp.maximum(amax, 1e-20))))
    g_sc[new] = (g * g_scale.astype(BF16)).astype(g_sc.dtype)
    gs_sc[new] = 1.0 / g_scale
    _gates(nofi_ref, c_ref, r2_ref, e2_ref, w_sc.at[old])

    @pl.when(e == pl.num_programs(1) - 1)
    def _():
        o_ref[...] = acc_sc[...].T.astype(o_ref.dtype)


def _experts(u_q, vt_q, ht_q, inv_scale, v_inv_scale, nofi_t, c_t, r2, e2, layer=None):
    d, n = ht_q.shape
    ne = u_q.shape[-2]
    tm = min(512, n)
    te = 1024
    gi = te // PEER_KEYS
    nblk = ne // te
    cur = lambda e: jnp.minimum(e, nblk - 1)
    nxt = lambda e: jnp.minimum(e + 1, nblk - 1)
    prev = lambda e: jnp.maximum(e - 1, 0)
    ispec0 = pl.BlockSpec((gi, PEER_HEADS, tm), lambda b, e: (0, 0, b))
    ispec = pl.BlockSpec((gi, PEER_HEADS, tm), lambda b, e: (nxt(e), 0, b))
    jspec = pl.BlockSpec((PEER_HEADS, PEER_KEYS, tm), lambda b, e: (0, 0, b))
    return pl.pallas_call(
        _expert_kernel,
        grid=(n // tm, nblk + 1),
        in_specs=[_wspec(layer, (te, d), lambda b, e: (cur(e), 0)),
                  _wspec(layer, (d, te), lambda b, e: (0, prev(e))),
                  pl.BlockSpec((d, tm), lambda b, e: (0, b)),
                  pl.BlockSpec((1, 1), lambda b, e: (0, 0)),
                  pl.BlockSpec((1, 1), lambda b, e: (0, 0)),
                  ispec0, ispec0, ispec, ispec, jspec, jspec],
        out_specs=pl.BlockSpec((tm, d), lambda b, e: (b, 0)),
        out_shape=jax.ShapeDtypeStruct((n, d), BF16),
        scratch_shapes=[pltpu.VMEM((d, tm), F32), pltpu.VMEM((2, te, tm), F8),
                        pltpu.VMEM((2, te, tm), BF16), pltpu.VMEM((2, 1, 1), F32)],
        compiler_params=_cparams(2),
        name="peer_experts",
    )(u_q, vt_q, ht_q, inv_scale, v_inv_scale, nofi_t, c_t, nofi_t, c_t, r2, e2)


def _ple_kernel(h_ref, wg_ref, inv_ref, p_ref, wp_ref, o_ref):
    g = jnp.dot(h_ref[...], wg_ref[...], preferred_element_type=F32) * inv_ref[...]
    pp = jnp.dot(p_ref[...], wp_ref[...], preferred_element_type=F32)
    o_ref[...] = (jax.nn.sigmoid(g) * pp).astype(o_ref.dtype)


def _ple(h_q, w_pg_q, inv_scale, p_b, w_pe, layer):
    n, d = h_q.shape
    kp = p_b.shape[-1]
    tm, tn = min(1024, n), 1024
    return pl.pallas_call(
        _ple_kernel,
        grid=(n // tm, d // tn),
        in_specs=[pl.BlockSpec((tm, d), lambda i, j: (i, 0)),
                  _wspec(layer, (d, tn), lambda i, j: (0, j)),
                  pl.BlockSpec((1, 1), lambda i, j: (0, 0)),
                  _wspec(layer, (tm, kp), lambda i, j: (i, 0)),
                  _wspec(layer, (kp, tn), lambda i, j: (0, j))],
        out_specs=pl.BlockSpec((tm, tn), lambda i, j: (i, j)),
        out_shape=jax.ShapeDtypeStruct((n, d), BF16),
        compiler_params=_cparams(2),
        name="ple",
    )(h_q, w_pg_q, inv_scale, p_b, w_pe)


def _pow2_scale(amax):
    return jnp.exp2(jnp.floor(jnp.log2(F8_TARGET_MAX / jnp.maximum(amax, 1e-30))))


def _ln_out_scale(g, b, d):
    return _pow2_scale(math.sqrt(d) * jnp.max(jnp.abs(g), axis=1) + jnp.max(jnp.abs(b), axis=1))


def _mixer_weights(w_in, w_uq, w_ukv, b_f):
    depth, d, _ = w_in.shape
    sizes = (MLA_Q_RANK, MLA_KV_RANK, MLA_ROPE, FOX_HEADS * FOX_DIM, FOX_HEADS * FOX_DIM,
             FOX_HEADS * FOX_DIM, FOX_HEADS, d, d)
    offs = np.concatenate([[0], np.cumsum(sizes)])
    cq, ckv, kr, _, _, _, fl, _, _ = [w_in[:, :, offs[t]:offs[t + 1]] for t in range(9)]
    zpad = lambda c: jnp.zeros((depth, d, c), w_in.dtype)
    w_small = jnp.concatenate([cq, ckv, kr, zpad(LANES - MLA_ROPE), fl, zpad(LANES - FOX_HEADS)],
                              axis=2).astype(BF16)
    w_fox = w_in[:, :, offs[3]:offs[6]].astype(BF16)
    w_gate = w_in[:, :, offs[7]:offs[9]]
    gate_scale = _pow2_scale(jnp.max(jnp.abs(w_gate), axis=(1, 2)))
    w_gate_q = (w_gate * gate_scale[:, None, None]).astype(F8)
    colscale = np.ones((1, w_fox.shape[2]), np.float32)
    colscale[0, :FOX_HEADS * FOX_DIM] = FOX_DIM ** -0.5 * LOG2E
    qk = MLA_NOPE + MLA_ROPE
    wq3 = w_uq.reshape(depth, MLA_Q_RANK, MLA_HEADS, qk)
    wqn = wq3[..., :MLA_NOPE].reshape(depth, MLA_Q_RANK, MLA_HEADS * MLA_NOPE).astype(BF16)
    wqr = jnp.pad(wq3[..., MLA_NOPE:], ((0, 0), (0, 0), (0, 0), (0, LANES - MLA_ROPE))
                  ).reshape(depth, MLA_Q_RANK, MLA_HEADS * LANES).astype(BF16)
    wkv3 = w_ukv.reshape(depth, MLA_KV_RANK, MLA_HEADS, MLA_NOPE + MLA_V)
    wuk = wkv3[..., :MLA_NOPE].reshape(depth, MLA_KV_RANK, MLA_HEADS * MLA_NOPE).astype(BF16)
    wuv = wkv3[..., MLA_NOPE:].reshape(depth, MLA_KV_RANK, MLA_HEADS * MLA_V).astype(BF16)
    bf_rows = jnp.pad(b_f, ((0, 0), (0, LANES - FOX_HEADS))).reshape(depth, 1, LANES)
    return w_small, w_fox, w_gate_q, gate_scale, jnp.asarray(colscale), wqn, wqr, wuk, wuv, bf_rows


def kernel(x, p, positions, w_in, g_cq, g_ckv, w_uq, w_ukv, b_f, w_ya, w_yb, w_o, ln1_g, ln1_b,
           w_pq, sub_k1, sub_k2, u_tab, v_tab, w_pg, w_pe, ln2_g, ln2_b):
    bsz, seq, d = x.shape
    depth = w_in.shape[0]
    n = bsz * seq
    assert bsz == 1, "token-major layout assumes a single sequence"
    alpha = (2 * depth) ** 0.25

    h = x.reshape(n, d)
    h_b = h.astype(BF16)
    tabs = _rope_tables(positions.reshape(n, 1))

    (w_small, w_fox, w_gate_q, gate_scale, colscale, wqn, wqr, wuk, wuv, bf_rows) = _mixer_weights(
        w_in, w_uq, w_ukv, b_f)
    w_ya_b, w_yb_b, w_o_b = w_ya.astype(BF16), w_yb.astype(BF16), w_o.astype(BF16)
    w_pq_b, w_pe_b = w_pq.astype(BF16), w_pe.astype(BF16)
    v_scale = _pow2_scale(jnp.max(jnp.abs(v_tab), axis=(1, 2)))
    vt_q = (v_tab.transpose(0, 2, 1) * v_scale[:, None, None]).astype(F8)
    p_b = p.reshape(depth, n, -1).astype(BF16)
    u_scale = _pow2_scale(jnp.max(jnp.abs(u_tab), axis=(1, 2)))
    u_q = (u_tab * u_scale[:, None, None]).astype(F8)
    pg_scale = _pow2_scale(jnp.max(jnp.abs(w_pg), axis=(1, 2)))
    w_pg_q = (w_pg * pg_scale[:, None, None]).astype(F8)
    h1_scale = _ln_out_scale(ln1_g, ln1_b, d)
    h2_scale = _ln_out_scale(ln2_g, ln2_b, d)
    in_scale = _pow2_scale(jnp.max(jnp.abs(h)))
    h_q = (h * in_scale).astype(F8)

    for i in range(depth):
        zs = _mm(h_b, w_small, F32, 1024, w_small.shape[-1] // 2, name="in_proj_small", layer=i)
        zf = _mm(h_b, w_fox, BF16, 1024, 1024, colscale, name="in_proj_fox", layer=i)
        gate_inv = jnp.full((1, 2 * d), 1.0, F32) / (in_scale * gate_scale[i])
        zg = _mm(h_q, w_gate_q, BF16, 1024, 1024, gate_inv, name="in_proj_gate", layer=i)
        q_mla = _mla_q(zs, g_cq[i].reshape(1, -1), wqn[i], wqr[i], tabs)
        k_mla, v_mla = _mla_kv(zs, g_ckv[i].reshape(1, -1), wuk[i], wuv[i], tabs)
        y_a = _attention(q_mla, k_mla, v_mla, MLA_HEADS, 2 * LANES, MLA_V, 0, 0, 0, "attn_mla")
        q_fox, k_fox = _fox_pack(zf, _fox_cum(zs, bf_rows[i]))
        y_b = _attention(q_fox, k_fox, zf, FOX_HEADS, 2 * LANES, FOX_DIM, 0, 0, 2 * FOX_HEADS,
                         "attn_fox")
        merged = _merge(y_a, y_b, w_ya_b, w_yb_b, zg, 0, d, i)
        mix = _mm(merged, w_o_b, BF16, 1024, 1024, name="out_proj", layer=i)
        h, h_b, h1_q, ht_q = _ln(h, [mix], ln1_g[i].reshape(1, d), ln1_b[i].reshape(1, d), alpha,
                                 h1_scale[i].reshape(1, 1), with_row=True, with_t=True)

        qp = _mm(h_b, w_pq_b, F32, 1024, 1024, name="peer_query", layer=i)
        nofi, c, r2, e2 = _route(qp, sub_k1[i], sub_k2[i])
        ffn = _experts(u_q, vt_q, ht_q, (1.0 / (u_scale[i] * h1_scale[i])).reshape(1, 1),
                       (1.0 / v_scale[i]).reshape(1, 1),
                       nofi.transpose(1, 0, 2), c.transpose(1, 0, 2), r2, e2, i)
        ple = _ple(h1_q, w_pg_q, (1.0 / (h1_scale[i] * pg_scale[i])).reshape(1, 1), p_b, w_pe_b, i)
        if i + 1 < depth:
            in_scale = h2_scale[i]
            h, h_b, h_q = _ln(h, [ffn, ple], ln2_g[i].reshape(1, d), ln2_b[i].reshape(1, d), alpha,
                              in_scale.reshape(1, 1), with_row=True)
        else:
            h, h_b = _ln(h, [ffn, ple], ln2_g[i].reshape(1, d), ln2_b[i].reshape(1, d), alpha)

    return h.reshape(bsz, seq, d)
```
